```python
import math
import jax
import jax.numpy as jnp
from jax import lax
import numpy as np

D_MODEL = 1024
BATCH = 16
SEQ = 2048
DEPTH = 4

MEM_LEN = 256
N_EVEN = (DEPTH + 1) // 2
N_ODD = DEPTH // 2
EPS = 1e-6

CONV_CH = D_MODEL // 2
CONV_TAPS = 3
RET_HEADS = 8
RET_DK = D_MODEL // 16
RET_DV = D_MODEL // 16
RET_QK = RET_HEADS * RET_DK
RET_V = RET_HEADS * RET_DV
RET_CHUNK = 128
RET_THETA = 10000.0
RET_MAX_RATE = 1.0 / 32.0
RET_MIN_RATE = 1.0 / 512.0
EVEN_SPLITS = (CONV_CH, CONV_CH, CONV_CH, RET_QK, RET_QK, RET_V, RET_V)
EVEN_IN = sum(EVEN_SPLITS)
EVEN_OUT = CONV_CH + RET_V
DIFF_HEADS = 8
DIFF_D = D_MODEL // 16
DIFF_QK = 2 * DIFF_HEADS * DIFF_D
DIFF_V = DIFF_HEADS * 2 * DIFF_D
ODD_SPLITS = (DIFF_QK, DIFF_QK, DIFF_V)
ODD_IN = sum(ODD_SPLITS)
ROPE_DIM = DIFF_D // 4
ROPE_THETA = 500000.0
Q_BLOCK = 128
XA_HEADS = 4
XA_HD = D_MODEL // XA_HEADS
D_FF = 2816
N_EXPERTS = 8
TOP_K = 2
D_FF_EXPERT = 3584

kernel_name = "hybrid_conv_retention_diffattn_moe_encoder"


def rms_norm(x, g):
    x32 = x.astype(jnp.float32)
    y = x32 * lax.rsqrt(jnp.mean(x32 * x32, axis=-1, keepdims=True) + EPS)
    return (y * g.astype(jnp.float32)).astype(x.dtype)


def group_rms_norm(x, g):
    H, d = x.shape[1], x.shape[3]
    x32 = x.astype(jnp.float32)
    y = x32 * lax.rsqrt(jnp.mean(x32 * x32, axis=-1, keepdims=True) + EPS)
    return y * g.astype(jnp.float32).reshape(1, H, 1, d)


def split_cols(x, sizes):
    out, start = [], 0
    for s in sizes:
        out.append(x[..., start:start + s])
        start += s
    return out


def rotary(x, positions, rot_dim, theta):
    half = rot_dim // 2
    inv_freq = jnp.exp(-math.log(theta) * jnp.arange(half, dtype=jnp.float32) * (2.0 / rot_dim))
    ang = positions.astype(jnp.float32)[:, None, :, None] * inv_freq
    cos, sin = jnp.cos(ang), jnp.sin(ang)
    x32 = x.astype(jnp.float32)
    x1, x2 = x32[..., :half], x32[..., half:rot_dim]
    out = jnp.concatenate([x1 * cos - x2 * sin, x1 * sin + x2 * cos, x32[..., rot_dim:]], axis=-1)
    return out.astype(x.dtype)


def short_conv_mixer(h, gate_b, gate_c, conv_w):
    u = gate_c * h
    L = u.shape[1]
    up = jnp.pad(u, ((0, 0), (1, 1), (0, 0)))
    conv = conv_w[0] * up[:, :L] + conv_w[1] * up[:, 1:L + 1] + conv_w[2] * up[:, 2:]
    return gate_b * conv


def retention_direction(q, k, v, log_gamma, strict):
    B, H, L, dk = q.shape
    dv = v.shape[-1]
    C = RET_CHUNK
    NC = L // C
    qc = q.astype(jnp.float32).reshape(B, H, NC, C, dk)
    kc = k.astype(jnp.float32).reshape(B, H, NC, C, dk)
    vc = v.astype(jnp.float32).reshape(B, H, NC, C, dv)
    idx = jnp.arange(C, dtype=jnp.float32)
    lg = log_gamma[:, None]
    dist = idx[:, None] - idx[None, :]
    mask = dist > 0 if strict else dist >= 0
    decay = jnp.where(mask, jnp.exp(lg[:, :, None] * jnp.where(mask, dist, 0.0)), 0.0)
    scores = jnp.einsum('bhnid,bhnjd->bhnij', qc, kc) * decay[None, :, None]
    intra = jnp.einsum('bhnij,bhnjv->bhniv', scores, vc)
    k_dec = jnp.exp(lg * (C - 1.0 - idx))
    kv = jnp.einsum('bhnjd,hj,bhnjv->nbhdv', kc, k_dec, vc)
    chunk_decay = jnp.exp(lg * C)[:, :, None]

    def step(state, kv_n):
        return chunk_decay * state + kv_n, state

    _, prev = lax.scan(step, jnp.zeros((B, H, dk, dv), jnp.float32), kv)
    q_dec = jnp.exp(lg * (idx + 1.0))
    inter = jnp.einsum('bhnid,hi,nbhdv->bhniv', qc, q_dec, prev)
    return (intra + inter).reshape(B, H, L, dv)


def bidirectional_retention(q, k, v, p_fwd, p_bwd):
    lg_f = -jnp.exp(p_fwd.astype(jnp.float32))
    lg_b = -jnp.exp(p_bwd.astype(jnp.float32))
    fwd = retention_direction(q, k, v, lg_f, False)
    bwd = retention_direction(jnp.flip(q, 2), jnp.flip(k, 2), jnp.flip(v, 2), lg_b, True)
    return fwd + jnp.flip(bwd, 2)


def even_token_mixer(h, positions, w_in, conv_w, p_fwd, p_bwd, gn_g, w_out):
    B, L, _ = h.shape
    gb, gc, hc, q, k, v, g = split_cols(h @ w_in, EVEN_SPLITS)
    a = short_conv_mixer(hc, gb, gc, conv_w)

    def to_heads(t, d):
        return t.reshape(B, L, RET_HEADS, d).transpose(0, 2, 1, 3)

    q = rotary(to_heads(q, RET_DK), positions, RET_DK, RET_THETA)
    k = rotary(to_heads(k, RET_DK), positions, RET_DK, RET_THETA) * (RET_DK ** -0.5)
    o = bidirectional_retention(q, k, to_heads(v, RET_DV), p_fwd, p_bwd)
    o = group_rms_norm(o, gn_g).transpose(0, 2, 1, 3).reshape(B, L, RET_V)
    r = (jax.nn.silu(g.astype(jnp.float32)) * o).astype(h.dtype)
    return jnp.concatenate([a, r], axis=-1) @ w_out


def diff_attention_mixer(h, positions, w_in, lam_q1, lam_k1, lam_q2, lam_k2, gn_g, w_out, lambda_init):
    B, L, _ = h.shape
    q, k, v = split_cols(h @ w_in, ODD_SPLITS)
    q = rotary(q.reshape(B, L, 2 * DIFF_HEADS, DIFF_D).transpose(0, 2, 1, 3), positions, ROPE_DIM, ROPE_THETA)
    k = rotary(k.reshape(B, L, 2 * DIFF_HEADS, DIFF_D).transpose(0, 2, 1, 3), positions, ROPE_DIM, ROPE_THETA)
    v = v.reshape(B, L, DIFF_HEADS, 2 * DIFF_D).transpose(0, 2, 1, 3).astype(jnp.float32)
    f32 = jnp.float32
    lam = (jnp.exp(jnp.sum(lam_q1.astype(f32) * lam_k1.astype(f32)))
           - jnp.exp(jnp.sum(lam_q2.astype(f32) * lam_k2.astype(f32))) + lambda_init)
    nb = L // Q_BLOCK
    q_blocks = q.reshape(B, 2 * DIFF_HEADS, nb, Q_BLOCK, DIFF_D).transpose(2, 0, 1, 3, 4)
    scale = DIFF_D ** -0.5

    def attend(qb):
        s = jnp.einsum('bgqd,bgkd->bgqk', qb, k).astype(f32) * scale
        p = jax.nn.softmax(s, axis=-1).reshape(B, DIFF_HEADS, 2, Q_BLOCK, L)
        return jnp.einsum('bhqk,bhkv->bhqv', p[:, :, 0] - lam * p[:, :, 1], v)

    o = lax.map(attend, q_blocks)
    o = o.transpose(1, 2, 0, 3, 4).reshape(B, DIFF_HEADS, L, 2 * DIFF_D)
    o = group_rms_norm(o, gn_g) * (1.0 - lambda_init)
    o = o.transpose(0, 2, 1, 3).reshape(B, L, DIFF_V).astype(h.dtype)
    return o @ w_out


def memory_cross_attention(h, mem_n, wq, wkv, wo):
    B, L, D = h.shape
    M = mem_n.shape[1]
    q = (h @ wq).reshape(B, L, XA_HEADS, XA_HD)
    k, v = split_cols(mem_n @ wkv, (D, D))
    k = k.reshape(B, M, XA_HEADS, XA_HD)
    v = v.reshape(B, M, XA_HEADS, XA_HD).astype(jnp.float32)
    s = jnp.einsum('blhd,bmhd->bhlm', q, k).astype(jnp.float32) * (XA_HD ** -0.5)
    p = jax.nn.softmax(s, axis=-1)
    o = jnp.einsum('bhlm,bmhd->blhd', p, v).reshape(B, L, D).astype(h.dtype)
    return o @ wo


def swiglu(h, w_gate, w_up, w_down):
    return (jax.nn.silu(h @ w_gate) * (h @ w_up)) @ w_down


def moe_swiglu(h, router_w, w_gate, w_up, w_down):
    B, L, D = h.shape
    t = h.reshape(B * L, D)
    logits = (t @ router_w).astype(jnp.float32)
    top_v, top_i = lax.top_k(logits, TOP_K)
    top_w = jax.nn.softmax(top_v, axis=-1)
    gates = jnp.sum(jax.nn.one_hot(top_i, N_EXPERTS, dtype=jnp.float32) * top_w[..., None], axis=1)
    out = jnp.zeros((B * L, D), jnp.float32)
    for e in range(N_EXPERTS):
        out = out + gates[:, e:e + 1] * swiglu(t, w_gate[e], w_up[e], w_down[e])
    return out.astype(h.dtype).reshape(B, L, D)


def setup_inputs(seed: int = 0) -> dict:
    key = jax.random.key(seed)
    ks = iter(jax.random.split(key, 48))
    f32 = jnp.float32

    def nrm(shape, fan_in):
        return jax.random.normal(next(ks), shape, f32) * (fan_in ** -0.5)

    def gain(shape):
        return 1.0 + 0.01 * jax.random.normal(next(ks), shape, f32)

    x = jax.random.normal(next(ks), (BATCH, SEQ, D_MODEL), f32)
    mem = jax.random.normal(next(ks), (BATCH, MEM_LEN, D_MODEL), f32)
    offsets = jax.random.randint(next(ks), (BATCH, 1), 0, SEQ, dtype=jnp.int32)
    positions = offsets + jnp.arange(SEQ, dtype=jnp.int32)[None, :]
    rates = jnp.exp(jnp.linspace(math.log(RET_MAX_RATE), math.log(RET_MIN_RATE), RET_HEADS)).astype(f32)
    base = jnp.log(-jnp.log1p(-rates))
    return {
        "x": x,
        "mem": mem,
        "positions": positions,
        "mem_norm_g": gain((D_MODEL,)),
        "final_norm_g": gain((D_MODEL,)),
        "mix_norm_g": gain((DEPTH, D_MODEL)),
        "xa_norm_g": gain((DEPTH, D_MODEL)),
        "ffn_norm_g": gain((DEPTH, D_MODEL)),
        "xa_wq": nrm((DEPTH, D_MODEL, D_MODEL), D_MODEL),
        "xa_wkv": nrm((DEPTH, D_MODEL, 2 * D_MODEL), D_MODEL),
        "xa_wo": nrm((DEPTH, D_MODEL, D_MODEL), D_MODEL),
        "ev_w_in": nrm((N_EVEN, D_MODEL, EVEN_IN), D_MODEL),
        "ev_conv_w": nrm((N_EVEN, CONV_TAPS, CONV_CH), CONV_TAPS),
        "ev_ret_decay_f": base[None, :] + 0.05 * jax.random.normal(next(ks), (N_EVEN, RET_HEADS), f32),
        "ev_ret_decay_b": base[None, :] + 0.05 * jax.random.normal(next(ks), (N_EVEN, RET_HEADS), f32),
        "ev_ret_gn_g": gain((N_EVEN, RET_V)),
        "ev_w_out": nrm((N_EVEN, EVEN_OUT, D_MODEL), EVEN_OUT),
        "ffd_w_gate": nrm((N_EVEN, D_MODEL, D_FF), D_MODEL),
        "ffd_w_up": nrm((N_EVEN, D_MODEL, D_FF), D_MODEL),
        "ffd_w_down": nrm((N_EVEN, D_FF, D_MODEL), D_FF),
        "od_w_in": nrm((N_ODD, D_MODEL, ODD_IN), D_MODEL),
        "od_lam_q1": 0.1 * jax.random.normal(next(ks), (N_ODD, DIFF_D), f32),
        "od_lam_k1": 0.1 * jax.random.normal(next(ks), (N_ODD, DIFF_D), f32),
        "od_lam_q2": 0.1 * jax.random.normal(next(ks), (N_ODD, DIFF_D), f32),
        "od_lam_k2": 0.1 * jax.random.normal(next(ks), (N_ODD, DIFF_D), f32),
        "od_gn_g": gain((N_ODD, DIFF_V)),
        "od_w_out": nrm((N_ODD, DIFF_V, D_MODEL), DIFF_V),
        "moe_router": nrm((N_ODD, D_MODEL, N_EXPERTS), D_MODEL),
        "moe_w_gate": nrm((N_ODD, N_EXPERTS, D_MODEL, D_FF_EXPERT), D_MODEL),
        "moe_w_up": nrm((N_ODD, N_EXPERTS, D_MODEL, D_FF_EXPERT), D_MODEL),
        "moe_w_down": nrm((N_ODD, N_EXPERTS, D_FF_EXPERT, D_MODEL), D_FF_EXPERT),
    }


def reference(x, mem, positions, mem_norm_g, final_norm_g, mix_norm_g, xa_norm_g, ffn_norm_g,
              xa_wq, xa_wkv, xa_wo, ev_w_in, ev_conv_w, ev_ret_decay_f, ev_ret_decay_b, ev_ret_gn_g,
              ev_w_out, ffd_w_gate, ffd_w_up, ffd_w_down, od_w_in, od_lam_q1, od_lam_k1, od_lam_q2,
              od_lam_k2, od_gn_g, od_w_out, moe_router, moe_w_gate, moe_w_up, moe_w_down):
    mem_n = rms_norm(mem, mem_norm_g)
    for i in range(DEPTH):
        j = i // 2
        h = rms_norm(x, mix_norm_g[i])
        if i % 2 == 0:
            x = x + even_token_mixer(h, positions, ev_w_in[j], ev_conv_w[j], ev_ret_decay_f[j],
                                     ev_ret_decay_b[j], ev_ret_gn_g[j], ev_w_out[j])
        else:
            lambda_init = 0.8 - 0.6 * math.exp(-0.3 * i)
            x = x + diff_attention_mixer(h, positions, od_w_in[j], od_lam_q1[j], od_lam_k1[j],
                                         od_lam_q2[j], od_lam_k2[j], od_gn_g[j], od_w_out[j], lambda_init)
        x = x + memory_cross_attention(rms_norm(x, xa_norm_g[i]), mem_n, xa_wq[i], xa_wkv[i], xa_wo[i])
        h = rms_norm(x, ffn_norm_g[i])
        if i % 2 == 0:
            x = x + swiglu(h, ffd_w_gate[j], ffd_w_up[j], ffd_w_down[j])
        else:
            x = x + moe_swiglu(h, moe_router[j], moe_w_gate[j], moe_w_up[j], moe_w_down[j])
    return rms_norm(x, final_norm_g)
```

```python
import functools
import math

import jax
import jax.numpy as jnp
from jax import lax
from jax.experimental import pallas as pl
from jax.experimental.pallas import tpu as pltpu

F32 = jnp.float32
BF16 = jnp.bfloat16
EPS = 1e-6
LANES = 128

RET_HEADS = 8
RET_THETA = 10000.0
RET_CHUNK = 256
DIFF_HEADS = 8
ROPE_THETA = 500000.0
XA_HEADS = 4
N_EXPERTS = 8

VMEM_LIMIT = 56 * 1024 * 1024


def _params(*sem):
    return pltpu.CompilerParams(dimension_semantics=sem, vmem_limit_bytes=VMEM_LIMIT)


def _rms(x, g):
    ms = jnp.mean(x * x, axis=-1, keepdims=True)
    return x * lax.rsqrt(ms + EPS) * g


def _dot(a, b):
    return jnp.dot(a, b, preferred_element_type=F32)


def _dot_nt(a, b):
    return lax.dot_general(a, b, (((1,), (1,)), ((), ())), preferred_element_type=F32)


def _rotate_pairs(x, cos_t, sin_t, half):
    lane = lax.broadcasted_iota(jnp.int32, x.shape, 1) % 64
    partner = jnp.where(lane < half, pltpu.roll(x, LANES - half, axis=1), pltpu.roll(x, half, axis=1))
    return x * cos_t + partner * sin_t


def _norm_proj_kernel(x_ref, g_ref, w_ref, o_ref, hn_ref):
    @pl.when(pl.program_id(1) == 0)
    def _():
        hn_ref[...] = _rms(x_ref[...], g_ref[...]).astype(BF16)

    o_ref[...] = _dot(hn_ref[...], w_ref[...]).astype(o_ref.dtype)


def norm_proj(x, g, w, *, tm=1024, tn=512):
    T, D = x.shape
    N = w.shape[1]
    tm, tn = min(tm, T), min(tn, N)
    assert T % tm == 0 and N % tn == 0
    return pl.pallas_call(
        _norm_proj_kernel,
        grid=(T // tm, N // tn),
        in_specs=[pl.BlockSpec((tm, D), lambda i, j: (i, 0)),
                  pl.BlockSpec((1, D), lambda i, j: (0, 0)),
                  pl.BlockSpec((D, tn), lambda i, j: (0, j))],
        out_specs=pl.BlockSpec((tm, tn), lambda i, j: (i, j)),
        out_shape=jax.ShapeDtypeStruct((T, N), BF16),
        scratch_shapes=[pltpu.VMEM((tm, D), BF16)],
        compiler_params=_params("parallel", "arbitrary"),
        name="norm_proj",
    )(x, g.reshape(1, D), w)


def _proj_residual_kernel(*refs, n_parts):
    parts, ws = refs[:n_parts], refs[n_parts:2 * n_parts]
    x_ref, o_ref = refs[2 * n_parts], refs[2 * n_parts + 1]
    acc = x_ref[...]
    for a_ref, w_ref in zip(parts, ws):
        acc = acc + _dot(a_ref[...], w_ref[...])
    o_ref[...] = acc


def proj_residual(parts, ws, x, *, tm=1024):
    T, D = x.shape
    tm = min(tm, T)
    assert T % tm == 0
    n = len(parts)
    in_specs = [pl.BlockSpec((tm, a.shape[1]), lambda i: (i, 0)) for a in parts]
    in_specs += [pl.BlockSpec(w.shape, lambda i: (0, 0)) for w in ws]
    in_specs += [pl.BlockSpec((tm, D), lambda i: (i, 0))]
    return pl.pallas_call(
        functools.partial(_proj_residual_kernel, n_parts=n),
        grid=(T // tm,),
        in_specs=in_specs,
        out_specs=pl.BlockSpec((tm, D), lambda i: (i, 0)),
        out_shape=jax.ShapeDtypeStruct((T, D), F32),
        compiler_params=_params("parallel"),
        name="proj_residual",
    )(*parts, *ws, x)


def _conv_kernel(gb_ref, gc_ref, h_ref, w_ref, o_ref):
    u = gc_ref[0].astype(F32) * h_ref[0].astype(F32)
    L = u.shape[0]
    row = lax.broadcasted_iota(jnp.int32, u.shape, 0)
    prev = jnp.where(row == 0, 0.0, pltpu.roll(u, 1, axis=0))
    nxt = jnp.where(row == L - 1, 0.0, pltpu.roll(u, L - 1, axis=0))
    conv = w_ref[0:1, :] * prev + w_ref[1:2, :] * u + w_ref[2:3, :] * nxt
    o_ref[0] = (gb_ref[0].astype(F32) * conv).astype(BF16)


def short_conv(proj, conv_w, *, ch):
    B, L, _ = proj.shape
    col = lambda c: pl.BlockSpec((1, L, ch), lambda b: (b, 0, c))
    return pl.pallas_call(
        _conv_kernel,
        grid=(B,),
        in_specs=[col(0), col(1), col(2), pl.BlockSpec((3, ch), lambda b: (0, 0))],
        out_specs=pl.BlockSpec((1, L, ch), lambda b: (b, 0, 0)),
        out_shape=jax.ShapeDtypeStruct((B, L, ch), BF16),
        compiler_params=_params("parallel"),
        name="short_conv",
    )(proj, proj, proj, conv_w)


def _retention_kernel(q_ref, k_ref, v_ref, g_ref, cos_ref, sin_ref, dmat_ref, qdec_ref, kdecf_ref,
                      kdecb_ref, cdecf_ref, cdecb_ref, gn_ref, o_ref, qrot_ref, krot_ref, bst_ref,
                      fstate_ref, bstate_ref, *, chunk, dk):
    L = q_ref.shape[1]
    n_pairs = q_ref.shape[2] // LANES
    nc = L // chunk
    C = chunk
    cos_t, sin_t = cos_ref[0], sin_ref[0]

    for p in range(n_pairs):
        cols = slice(p * LANES, (p + 1) * LANES)
        qrot_ref[:, cols] = _rotate_pairs(q_ref[0, :, cols].astype(F32), cos_t, sin_t, dk // 2).astype(BF16)
        kr = _rotate_pairs(k_ref[0, :, cols].astype(F32), cos_t, sin_t, dk // 2)
        krot_ref[:, cols] = (kr * (dk ** -0.5)).astype(BF16)

    lane_c = lax.broadcasted_iota(jnp.int32, (C, LANES), 1)
    first_head = lane_c < dk
    r_i = lax.broadcasted_iota(jnp.int32, (LANES, LANES), 0) // dk
    c_i = lax.broadcasted_iota(jnp.int32, (LANES, LANES), 1) // dk
    same_head = r_i == c_i

    def kv_outer(kp, kdec, vp):
        kd_t = (kp.astype(F32) * kdec).T.astype(BF16)
        return jnp.where(same_head, _dot(kd_t, vp), 0.0)

    bstate_ref[...] = jnp.zeros_like(bstate_ref)

    def back_body(t, carry):
        i = nc - 1 - t
        rows = pl.ds(pl.multiple_of(i * C, C), C)
        for p in range(n_pairs):
            cols = slice(p * LANES, (p + 1) * LANES)
            bst_ref[i, p] = bstate_ref[p]
            upd = kv_outer(krot_ref[rows, cols], kdecb_ref[p], v_ref[0, rows, cols])
            bstate_ref[p] = cdecb_ref[p] * bstate_ref[p] + upd
        return carry

    lax.fori_loop(0, nc, back_body, 0)

    fstate_ref[...] = jnp.zeros_like(fstate_ref)

    def fwd_body(i, carry):
        rows = pl.ds(pl.multiple_of(i * C, C), C)
        for p in range(n_pairs):
            cols = slice(p * LANES, (p + 1) * LANES)
            qp, kp, vp = qrot_ref[rows, cols], krot_ref[rows, cols], v_ref[0, rows, cols]
            zero = jnp.zeros_like(qp)
            s0 = _dot_nt(jnp.where(first_head, qp, zero), kp) * dmat_ref[2 * p]
            s1 = _dot_nt(jnp.where(first_head, zero, qp), kp) * dmat_ref[2 * p + 1]
            intra = jnp.where(first_head, _dot(s0.astype(BF16), vp), _dot(s1.astype(BF16), vp))
            q32 = qp.astype(F32)
            qd = qdec_ref[p]
            qq = jnp.concatenate([q32 * qd[:, :LANES], q32 * qd[:, LANES:]], axis=1).astype(BF16)
            st = jnp.concatenate([fstate_ref[p], bst_ref[i, p]], axis=0).astype(BF16)
            o = intra + _dot(qq, st)
            o2 = o * o
            s_first = jnp.sum(jnp.where(first_head, o2, 0.0), axis=-1, keepdims=True)
            s_all = jnp.sum(o2, axis=-1, keepdims=True)
            ms = jnp.where(first_head, s_first, s_all - s_first) * (1.0 / dk)
            on = o * lax.rsqrt(ms + EPS) * gn_ref[:, cols]
            g = g_ref[0, rows, cols].astype(F32)
            o_ref[0, rows, cols] = (g * jax.nn.sigmoid(g) * on).astype(BF16)
            upd = kv_outer(kp, kdecf_ref[p], vp)
            fstate_ref[p] = cdecf_ref[p] * fstate_ref[p] + upd
        return carry

    lax.fori_loop(0, nc, fwd_body, 0)


def _retention_tables(p_fwd, p_bwd, C, dk):
    H = p_fwd.shape[0]
    lg_f = -jnp.exp(p_fwd.astype(F32))
    lg_b = -jnp.exp(p_bwd.astype(F32))
    idx = jnp.arange(C, dtype=F32)
    dist = idx[:, None] - idx[None, :]
    dmat = jnp.where(dist >= 0,
                     jnp.exp(lg_f[:, None, None] * jnp.maximum(dist, 0.0)),
                     jnp.exp(lg_b[:, None, None] * jnp.maximum(-dist, 0.0)))

    def lanes(per_head):
        t = jnp.repeat(per_head[:, :, None], dk, axis=2)
        return t.reshape(H // 2, 2, C, dk).transpose(0, 2, 1, 3).reshape(H // 2, C, 2 * dk)

    qdec_f = lanes(jnp.exp(lg_f[:, None] * (idx + 1.0)))
    qdec_b = lanes(jnp.exp(lg_b[:, None] * (C - idx)))
    kdec_f = lanes(jnp.exp(lg_f[:, None] * (C - 1.0 - idx)))
    kdec_b = lanes(jnp.exp(lg_b[:, None] * idx))
    qdec = jnp.concatenate([qdec_f, qdec_b], axis=2)

    def rows(lg):
        per_row = jnp.repeat(jnp.exp(lg * C).reshape(H // 2, 2), dk, axis=1)
        return jnp.broadcast_to(per_row[:, :, None], (H // 2, 2 * dk, 2 * dk))

    return dmat, qdec, kdec_f, kdec_b, rows(lg_f), rows(lg_b)


def _rope_tables(positions, rot_dim, theta, head_dim=64):
    half = rot_dim // 2
    inv_freq = jnp.exp(-math.log(theta) * jnp.arange(half, dtype=F32) * (2.0 / rot_dim))
    ang = positions.astype(F32)[:, :, None] * inv_freq
    cos, sin = jnp.cos(ang), jnp.sin(ang)
    B, L = positions.shape
    pad = head_dim - rot_dim
    cos_h = jnp.concatenate([cos, cos, jnp.ones((B, L, pad), F32)], axis=-1)
    sin_h = jnp.concatenate([-sin, sin, jnp.zeros((B, L, pad), F32)], axis=-1)
    return jnp.concatenate([cos_h, cos_h], axis=-1), jnp.concatenate([sin_h, sin_h], axis=-1)


def retention(proj, cos_t, sin_t, p_fwd, p_bwd, gn_g, *, col0, width, dk=64):
    B, L, _ = proj.shape
    C = min(RET_CHUNK, L)
    assert L % C == 0 and width % LANES == 0
    n_pairs = width // LANES
    nc = L // C
    tables = _retention_tables(p_fwd, p_bwd, C, dk)
    col = lambda c: pl.BlockSpec((1, L, width), lambda b: (b, 0, col0 + c))
    rope = pl.BlockSpec((1, L, LANES), lambda b: (b, 0, 0))
    full = lambda a: pl.BlockSpec(a.shape, lambda b: (0,) * a.ndim)
    gn = gn_g.reshape(1, width).astype(F32)
    return pl.pallas_call(
        functools.partial(_retention_kernel, chunk=C, dk=dk),
        grid=(B,),
        in_specs=[col(0), col(1), col(2), col(3), rope, rope] + [full(t) for t in tables] + [full(gn)],
        out_specs=pl.BlockSpec((1, L, width), lambda b: (b, 0, 0)),
        out_shape=jax.ShapeDtypeStruct((B, L, width), BF16),
        scratch_shapes=[pltpu.VMEM((L, width), BF16), pltpu.VMEM((L, width), BF16),
                        pltpu.VMEM((nc, n_pairs, LANES, LANES), F32),
                        pltpu.VMEM((n_pairs, LANES, LANES), F32),
                        pltpu.VMEM((n_pairs, LANES, LANES), F32)],
        compiler_params=_params("parallel"),
        name="retention",
    )(proj, proj, proj, proj, cos_t, sin_t, *tables, gn)


def _diff_attn_kernel(lam_ref, q_ref, k_ref, v_ref, cosk_ref, sink_ref, cosq_ref, sinq_ref, gn_ref,
                      o_ref, krot_ref, *, d, rot_half, post_scale):
    n_heads = q_ref.shape[2] // LANES

    @pl.when(pl.program_id(1) == 0)
    def _():
        for h in range(n_heads):
            cols = slice(h * LANES, (h + 1) * LANES)
            kr = _rotate_pairs(k_ref[0, :, cols].astype(F32), cosk_ref[0], sink_ref[0], rot_half)
            krot_ref[:, cols] = kr.astype(BF16)

    lam = lam_ref[0, 0]
    tq = q_ref.shape[1]
    first_map = lax.broadcasted_iota(jnp.int32, (tq, LANES), 1) < d

    def softmax(s):
        e = jnp.exp(s - jnp.max(s, axis=-1, keepdims=True))
        return e / jnp.sum(e, axis=-1, keepdims=True)

    for h in range(n_heads):
        cols = slice(h * LANES, (h + 1) * LANES)
        qr = _rotate_pairs(q_ref[0, :, cols].astype(F32), cosq_ref[0], sinq_ref[0], rot_half) * (d ** -0.5)
        kp = krot_ref[:, cols]
        p1 = softmax(_dot_nt(jnp.where(first_map, qr, 0.0).astype(BF16), kp))
        p2 = softmax(_dot_nt(jnp.where(first_map, 0.0, qr).astype(BF16), kp))
        o = _dot((p1 - lam * p2).astype(BF16), v_ref[0, :, cols])
        on = _rms(o, gn_ref[:, cols]) * post_scale
        o_ref[0, :, cols] = on.astype(BF16)


def diff_attention(proj, cos_t, sin_t, lam, gn_g, *, rot_dim, lambda_init, d=64, tq=256):
    B, L, N = proj.shape
    W = N // 3
    tq = min(tq, L)
    assert L % tq == 0
    col = lambda c, rows, f: pl.BlockSpec((1, rows, W), f(c))
    q_map = lambda c: (lambda b, i: (b, i, c))
    kv_map = lambda c: (lambda b, i: (b, 0, c))
    return pl.pallas_call(
        functools.partial(_diff_attn_kernel, d=d, rot_half=rot_dim // 2, post_scale=1.0 - lambda_init),
        grid=(B, L // tq),
        in_specs=[pl.BlockSpec(memory_space=pltpu.SMEM),
                  col(0, tq, q_map), col(1, L, kv_map), col(2, L, kv_map),
                  pl.BlockSpec((1, L, LANES), lambda b, i: (b, 0, 0)),
                  pl.BlockSpec((1, L, LANES), lambda b, i: (b, 0, 0)),
                  pl.BlockSpec((1, tq, LANES), lambda b, i: (b, i, 0)),
                  pl.BlockSpec((1, tq, LANES), lambda b, i: (b, i, 0)),
                  pl.BlockSpec((1, W), lambda b, i: (0, 0))],
        out_specs=pl.BlockSpec((1, tq, W), lambda b, i: (b, i, 0)),
        out_shape=jax.ShapeDtypeStruct((B, L, W), BF16),
        scratch_shapes=[pltpu.VMEM((L, W), BF16)],
        compiler_params=_params("parallel", "arbitrary"),
        name="diff_attention",
    )(lam.reshape(1, 1).astype(F32), proj, proj, proj, cos_t, sin_t, cos_t, sin_t,
      gn_g.reshape(1, W).astype(F32))


def _xattn_kernel(x_ref, g_ref, wq_ref, kv_ref, wo_ref, o_ref, *, heads):
    x = x_ref[0]
    D = x.shape[1]
    hd = D // heads
    hn = _rms(x, g_ref[...]).astype(BF16)
    q = (_dot(hn, wq_ref[...]) * (hd ** -0.5)).astype(BF16)
    outs = []
    for h in range(heads):
        kh = kv_ref[0, :, h * hd:(h + 1) * hd]
        vh = kv_ref[0, :, D + h * hd:D + (h + 1) * hd]
        s = _dot_nt(q[:, h * hd:(h + 1) * hd], kh)
        e = jnp.exp(s - jnp.max(s, axis=-1, keepdims=True))
        oh = _dot(e.astype(BF16), vh) / jnp.sum(e, axis=-1, keepdims=True)
        outs.append(oh.astype(BF16))
    o_ref[0] = x + _dot(jnp.concatenate(outs, axis=-1), wo_ref[...])


def cross_attention(x, g, wq, kv, wo, *, tq=512):
    B, L, D = x.shape
    M = kv.shape[1]
    tq = min(tq, L)
    assert L % tq == 0
    return pl.pallas_call(
        functools.partial(_xattn_kernel, heads=XA_HEADS),
        grid=(B, L // tq),
        in_specs=[pl.BlockSpec((1, tq, D), lambda b, i: (b, i, 0)),
                  pl.BlockSpec((1, D), lambda b, i: (0, 0)),
                  pl.BlockSpec((D, D), lambda b, i: (0, 0)),
                  pl.BlockSpec((1, M, 2 * D), lambda b, i: (b, 0, 0)),
                  pl.BlockSpec((D, D), lambda b, i: (0, 0))],
        out_specs=pl.BlockSpec((1, tq, D), lambda b, i: (b, i, 0)),
        out_shape=jax.ShapeDtypeStruct((B, L, D), F32),
        compiler_params=_params("parallel", "parallel"),
        name="cross_attention",
    )(x, g.reshape(1, D), wq, kv, wo)


def _swiglu_kernel(x_ref, g_ref, wg_ref, wu_ref, wd_ref, o_ref, hn_ref, acc_ref):
    f = pl.program_id(1)

    @pl.when(f == 0)
    def _():
        hn_ref[...] = _rms(x_ref[...], g_ref[...]).astype(BF16)
        acc_ref[...] = jnp.zeros_like(acc_ref)

    hn = hn_ref[...]
    a = _dot(hn, wg_ref[...])
    b = _dot(hn, wu_ref[...])
    acc_ref[...] += _dot((a * jax.nn.sigmoid(a) * b).astype(BF16), wd_ref[...])

    @pl.when(f == pl.num_programs(1) - 1)
    def _():
        o_ref[...] = x_ref[...] + acc_ref[...]


def swiglu_residual(x, g, w_gate, w_up, w_down, *, tm=1024, tf=256):
    T, D = x.shape
    FF = w_gate.shape[1]
    tm, tf = min(tm, T), min(tf, FF)
    assert T % tm == 0 and FF % tf == 0
    return pl.pallas_call(
        _swiglu_kernel,
        grid=(T // tm, FF // tf),
        in_specs=[pl.BlockSpec((tm, D), lambda i, f: (i, 0)),
                  pl.BlockSpec((1, D), lambda i, f: (0, 0)),
                  pl.BlockSpec((D, tf), lambda i, f: (0, f)),
                  pl.BlockSpec((D, tf), lambda i, f: (0, f)),
                  pl.BlockSpec((tf, D), lambda i, f: (f, 0))],
        out_specs=pl.BlockSpec((tm, D), lambda i, f: (i, 0)),
        out_shape=jax.ShapeDtypeStruct((T, D), F32),
        scratch_shapes=[pltpu.VMEM((tm, D), BF16), pltpu.VMEM((tm, D), F32)],
        compiler_params=_params("parallel", "arbitrary"),
        name="swiglu_residual",
    )(x, g.reshape(1, D), w_gate, w_up, w_down)


def _router_kernel(x_ref, g_ref, wr_ref, hn_ref, gates_ref, *, n_experts):
    hn = _rms(x_ref[...], g_ref[...])
    hn_ref[...] = hn.astype(BF16)
    logits = jnp.dot(hn, wr_ref[...], preferred_element_type=F32, precision=lax.Precision.HIGHEST)
    lane = lax.broadcasted_iota(jnp.int32, logits.shape, 1)
    neg = jnp.float32(-jnp.inf)
    logits = jnp.where(lane < n_experts, logits, neg)
    m1 = jnp.max(logits, axis=-1, keepdims=True)
    i1 = jnp.min(jnp.where(logits == m1, lane, LANES), axis=-1, keepdims=True)
    rest = jnp.where(lane == i1, neg, logits)
    m2 = jnp.max(rest, axis=-1, keepdims=True)
    i2 = jnp.min(jnp.where(rest == m2, lane, LANES), axis=-1, keepdims=True)
    e2 = jnp.exp(m2 - m1)
    w1 = 1.0 / (1.0 + e2)
    w2 = e2 / (1.0 + e2)
    gates_ref[...] = jnp.where(lane == i1, w1, jnp.where(lane == i2, w2, 0.0))


def route_tokens(x, g, router_w, *, tm=1024):
    T, D = x.shape
    E = router_w.shape[1]
    tm = min(tm, T)
    wr = jnp.zeros((D, LANES), F32).at[:, :E].set(router_w.astype(F32))
    return pl.pallas_call(
        functools.partial(_router_kernel, n_experts=E),
        grid=(T // tm,),
        in_specs=[pl.BlockSpec((tm, D), lambda i: (i, 0)),
                  pl.BlockSpec((1, D), lambda i: (0, 0)),
                  pl.BlockSpec((D, LANES), lambda i: (0, 0))],
        out_specs=[pl.BlockSpec((tm, D), lambda i: (i, 0)),
                   pl.BlockSpec((tm, LANES), lambda i: (i, 0))],
        out_shape=[jax.ShapeDtypeStruct((T, D), BF16), jax.ShapeDtypeStruct((T, LANES), F32)],
        compiler_params=_params("parallel"),
        name="route_tokens",
    )(x, g.reshape(1, D), wr)


def _moe_dense_kernel(hn_ref, gates_ref, wg_ref, wu_ref, wd_ref, x_ref, o_ref, acc_ref):
    e, f = pl.program_id(1), pl.program_id(2)

    @pl.when((e == 0) & (f == 0))
    def _():
        acc_ref[...] = jnp.zeros_like(acc_ref)

    gates = gates_ref[...]
    lane = lax.broadcasted_iota(jnp.int32, gates.shape, 1)
    ge = jnp.sum(jnp.where(lane == e, gates, 0.0), axis=-1, keepdims=True)
    hn = hn_ref[...]
    a = _dot(hn, wg_ref[0])
    b = _dot(hn, wu_ref[0])
    acc_ref[...] += _dot((a * jax.nn.sigmoid(a) * b * ge).astype(BF16), wd_ref[0])

    @pl.when((e == pl.num_programs(1) - 1) & (f == pl.num_programs(2) - 1))
    def _():
        o_ref[...] = x_ref[...] + acc_ref[...]


def moe_dense_residual(x, hn, gates, w_gate, w_up, w_down, *, tm=1024, tf=512):
    T, D = x.shape
    E, _, FF = w_gate.shape
    tm, tf = min(tm, T), min(tf, FF)
    assert T % tm == 0 and FF % tf == 0
    return pl.pallas_call(
        _moe_dense_kernel,
        grid=(T // tm, E, FF // tf),
        in_specs=[pl.BlockSpec((tm, D), lambda i, e, f: (i, 0)),
                  pl.BlockSpec((tm, LANES), lambda i, e, f: (i, 0)),
                  pl.BlockSpec((1, D, tf), lambda i, e, f: (e, 0, f)),
                  pl.BlockSpec((1, D, tf), lambda i, e, f: (e, 0, f)),
                  pl.BlockSpec((1, tf, D), lambda i, e, f: (e, f, 0)),
                  pl.BlockSpec((tm, D), lambda i, e, f: (i, 0))],
        out_specs=pl.BlockSpec((tm, D), lambda i, e, f: (i, 0)),
        out_shape=jax.ShapeDtypeStruct((T, D), F32),
        scratch_shapes=[pltpu.VMEM((tm, D), F32)],
        compiler_params=_params("parallel", "arbitrary", "arbitrary"),
        name="moe_dense",
    )(hn, gates, w_gate, w_up, w_down, x)


def moe_residual(x, g, router_w, w_gate, w_up, w_down):
    hn, gates = route_tokens(x, g, router_w)
    return moe_dense_residual(x, hn, gates, w_gate, w_up, w_down)


def _final_norm_kernel(x_ref, g_ref, o_ref):
    o_ref[...] = _rms(x_ref[...], g_ref[...])


def final_norm(x, g, *, tm=1024):
    T, D = x.shape
    tm = min(tm, T)
    return pl.pallas_call(
        _final_norm_kernel,
        grid=(T // tm,),
        in_specs=[pl.BlockSpec((tm, D), lambda i: (i, 0)), pl.BlockSpec((1, D), lambda i: (0, 0))],
        out_specs=pl.BlockSpec((tm, D), lambda i: (i, 0)),
        out_shape=jax.ShapeDtypeStruct((T, D), F32),
        compiler_params=_params("parallel"),
        name="final_norm",
    )(x, g.reshape(1, D))


def kernel(x, mem, positions, mem_norm_g, final_norm_g, mix_norm_g, xa_norm_g, ffn_norm_g, xa_wq, xa_wkv, xa_wo, ev_w_in, ev_conv_w, ev_ret_decay_f, ev_ret_decay_b, ev_ret_gn_g, ev_w_out, ffd_w_gate, ffd_w_up, ffd_w_down, od_w_in, od_lam_q1, od_lam_k1, od_lam_q2, od_lam_k2, od_gn_g, od_w_out, moe_router, moe_w_gate, moe_w_up, moe_w_down):
    B, L, D = x.shape
    M = mem.shape[1]
    T = B * L
    depth = mix_norm_g.shape[0]
    conv_ch = ev_conv_w.shape[2]
    ret_w = ev_ret_gn_g.shape[1]
    diff_d = od_lam_q1.shape[1]
    bf = lambda w: w.astype(BF16)

    ret_dk = ret_w // RET_HEADS
    ret_cos, ret_sin = _rope_tables(positions, ret_dk, RET_THETA)
    diff_cos, diff_sin = _rope_tables(positions, diff_d // 4, ROPE_THETA)
    mem2 = mem.reshape(B * M, D)
    xt = x.reshape(T, D)

    for i in range(depth):
        j = i // 2
        if i % 2 == 0:
            proj = norm_proj(xt, mix_norm_g[i], bf(ev_w_in[j])).reshape(B, L, -1)
            a = short_conv(proj, ev_conv_w[j].astype(F32), ch=conv_ch)
            r = retention(proj, ret_cos, ret_sin, ev_ret_decay_f[j], ev_ret_decay_b[j], ev_ret_gn_g[j],
                          col0=3 * conv_ch // ret_w, width=ret_w, dk=ret_dk)
            w_out = bf(ev_w_out[j])
            xt = proj_residual([a.reshape(T, conv_ch), r.reshape(T, ret_w)],
                               [w_out[:conv_ch], w_out[conv_ch:]], xt)
        else:
            lambda_init = 0.8 - 0.6 * math.exp(-0.3 * i)
            f32 = lambda v: v.astype(F32)
            lam = (jnp.exp(jnp.sum(f32(od_lam_q1[j]) * f32(od_lam_k1[j])))
                   - jnp.exp(jnp.sum(f32(od_lam_q2[j]) * f32(od_lam_k2[j]))) + lambda_init)
            proj = norm_proj(xt, mix_norm_g[i], bf(od_w_in[j])).reshape(B, L, -1)
            o = diff_attention(proj, diff_cos, diff_sin, lam, od_gn_g[j], rot_dim=diff_d // 4,
                               lambda_init=lambda_init, d=diff_d)
            xt = proj_residual([o.reshape(T, -1)], [bf(od_w_out[j])], xt)

        kv = norm_proj(mem2, mem_norm_g, bf(xa_wkv[i])).reshape(B, M, 2 * D)
        xt = cross_attention(xt.reshape(B, L, D), xa_norm_g[i], bf(xa_wq[i]), kv, bf(xa_wo[i])).reshape(T, D)

        if i % 2 == 0:
            xt = swiglu_residual(xt, ffn_norm_g[i], bf(ffd_w_gate[j]), bf(ffd_w_up[j]), bf(ffd_w_down[j]))
        else:
            xt = moe_residual(xt, ffn_norm_g[i], moe_router[j], bf(moe_w_gate[j]), bf(moe_w_up[j]),
                              bf(moe_w_down[j]))

    return final_norm(xt, final_norm_g).reshape(B, L, D)
```

```python
import functools
import math

import jax
import jax.numpy as jnp
from jax import lax
from jax.experimental import pallas as pl
from jax.experimental.pallas import tpu as pltpu

F32 = jnp.float32
BF16 = jnp.bfloat16
EPS = 1e-6
LANES = 128

RET_HEADS = 8
RET_THETA = 10000.0
RET_CHUNK = 256
DIFF_HEADS = 8
ROPE_THETA = 500000.0
XA_HEADS = 4
N_EXPERTS = 8

VMEM_LIMIT = 56 * 1024 * 1024


def _params(*sem):
    return pltpu.CompilerParams(dimension_semantics=sem, vmem_limit_bytes=VMEM_LIMIT)


def _rms(x, g):
    ms = jnp.mean(x * x, axis=-1, keepdims=True)
    return x * lax.rsqrt(ms + EPS) * g


def _dot(a, b):
    return jnp.dot(a, b, preferred_element_type=F32)


def _dot_nt(a, b):
    return lax.dot_general(a, b, (((1,), (1,)), ((), ())), preferred_element_type=F32)


def _rotate_pairs(x, cos_t, sin_t, half):
    lane = lax.broadcasted_iota(jnp.int32, x.shape, 1) % 64
    partner = jnp.where(lane < half, pltpu.roll(x, LANES - half, axis=1), pltpu.roll(x, half, axis=1))
    return x * cos_t + partner * sin_t


def _norm_proj_kernel(x_ref, g_ref, w_ref, o_ref, hn_ref):
    @pl.when(pl.program_id(1) == 0)
    def _():
        hn_ref[...] = _rms(x_ref[...], g_ref[...]).astype(BF16)

    o_ref[...] = _dot(hn_ref[...], w_ref[...]).astype(o_ref.dtype)


def norm_proj(x, g, w, *, tm=1024, tn=512):
    T, D = x.shape
    N = w.shape[1]
    tm, tn = min(tm, T), min(tn, N)
    assert T % tm == 0 and N % tn == 0
    return pl.pallas_call(
        _norm_proj_kernel,
        grid=(T // tm, N // tn),
        in_specs=[pl.BlockSpec((tm, D), lambda i, j: (i, 0)),
                  pl.BlockSpec((1, D), lambda i, j: (0, 0)),
                  pl.BlockSpec((D, tn), lambda i, j: (0, j))],
        out_specs=pl.BlockSpec((tm, tn), lambda i, j: (i, j)),
        out_shape=jax.ShapeDtypeStruct((T, N), BF16),
        scratch_shapes=[pltpu.VMEM((tm, D), BF16)],
        compiler_params=_params("parallel", "arbitrary"),
        name="norm_proj",
    )(x, g.reshape(1, D), w)


def _proj_residual_kernel(*refs, n_parts):
    parts, ws = refs[:n_parts], refs[n_parts:2 * n_parts]
    x_ref, o_ref = refs[2 * n_parts], refs[2 * n_parts + 1]
    acc = x_ref[...]
    for a_ref, w_ref in zip(parts, ws):
        acc = acc + _dot(a_ref[...], w_ref[...])
    o_ref[...] = acc


def proj_residual(parts, ws, x, *, tm=1024):
    T, D = x.shape
    tm = min(tm, T)
    assert T % tm == 0
    n = len(parts)
    in_specs = [pl.BlockSpec((tm, a.shape[1]), lambda i: (i, 0)) for a in parts]
    in_specs += [pl.BlockSpec(w.shape, lambda i: (0, 0)) for w in ws]
    in_specs += [pl.BlockSpec((tm, D), lambda i: (i, 0))]
    return pl.pallas_call(
        functools.partial(_proj_residual_kernel, n_parts=n),
        grid=(T // tm,),
        in_specs=in_specs,
        out_specs=pl.BlockSpec((tm, D), lambda i: (i, 0)),
        out_shape=jax.ShapeDtypeStruct((T, D), F32),
        compiler_params=_params("parallel"),
        name="proj_residual",
    )(*parts, *ws, x)


def _conv_kernel(gb_ref, gc_ref, h_ref, w_ref, o_ref):
    u = gc_ref[0].astype(F32) * h_ref[0].astype(F32)
    L = u.shape[0]
    row = lax.broadcasted_iota(jnp.int32, u.shape, 0)
    prev = jnp.where(row == 0, 0.0, pltpu.roll(u, 1, axis=0))
    nxt = jnp.where(row == L - 1, 0.0, pltpu.roll(u, L - 1, axis=0))
    conv = w_ref[0:1, :] * prev + w_ref[1:2, :] * u + w_ref[2:3, :] * nxt
    o_ref[0] = (gb_ref[0].astype(F32) * conv).astype(BF16)


def short_conv(proj, conv_w, *, ch):
    B, L, _ = proj.shape
    col = lambda c: pl.BlockSpec((1, L, ch), lambda b: (b, 0, c))
    return pl.pallas_call(
        _conv_kernel,
        grid=(B,),
        in_specs=[col(0), col(1), col(2), pl.BlockSpec((3, ch), lambda b: (0, 0))],
        out_specs=pl.BlockSpec((1, L, ch), lambda b: (b, 0, 0)),
        out_shape=jax.ShapeDtypeStruct((B, L, ch), BF16),
        compiler_params=_params("parallel"),
        name="short_conv",
    )(proj, proj, proj, conv_w)


def _retention_kernel(q_ref, k_ref, v_ref, g_ref, cos_ref, sin_ref, dmat_ref, qdec_ref, kdecf_ref,
                      kdecb_ref, cdecf_ref, cdecb_ref, gn_ref, o_ref, qrot_ref, krot_ref, bst_ref,
                      fstate_ref, bstate_ref, *, chunk, dk):
    L = q_ref.shape[1]
    n_pairs = q_ref.shape[2] // LANES
    nc = L // chunk
    C = chunk
    cos_t, sin_t = cos_ref[0], sin_ref[0]

    for p in range(n_pairs):
        cols = slice(p * LANES, (p + 1) * LANES)
        qrot_ref[:, cols] = _rotate_pairs(q_ref[0, :, cols].astype(F32), cos_t, sin_t, dk // 2).astype(BF16)
        kr = _rotate_pairs(k_ref[0, :, cols].astype(F32), cos_t, sin_t, dk // 2)
        krot_ref[:, cols] = (kr * (dk ** -0.5)).astype(BF16)

    lane_c = lax.broadcasted_iota(jnp.int32, (C, LANES), 1)
    first_head = lane_c < dk
    r_i = lax.broadcasted_iota(jnp.int32, (LANES, LANES), 0) // dk
    c_i = lax.broadcasted_iota(jnp.int32, (LANES, LANES), 1) // dk
    same_head = r_i == c_i

    def kv_outer(kp, kdec, vp):
        kd_t = (kp.astype(F32) * kdec).T.astype(BF16)
        return jnp.where(same_head, _dot(kd_t, vp), 0.0)

    bstate_ref[...] = jnp.zeros_like(bstate_ref)

    def back_body(t, carry):
        i = nc - 1 - t
        rows = pl.ds(pl.multiple_of(i * C, C), C)
        for p in range(n_pairs):
            cols = slice(p * LANES, (p + 1) * LANES)
            bst_ref[i, p] = bstate_ref[p]
            upd = kv_outer(krot_ref[rows, cols], kdecb_ref[p], v_ref[0, rows, cols])
            bstate_ref[p] = cdecb_ref[p] * bstate_ref[p] + upd
        return carry

    lax.fori_loop(0, nc, back_body, 0)

    fstate_ref[...] = jnp.zeros_like(fstate_ref)

    def fwd_body(i, carry):
        rows = pl.ds(pl.multiple_of(i * C, C), C)
        for p in range(n_pairs):
            cols = slice(p * LANES, (p + 1) * LANES)
            qp, kp, vp = qrot_ref[rows, cols], krot_ref[rows, cols], v_ref[0, rows, cols]
            zero = jnp.zeros_like(qp)
            s0 = _dot_nt(jnp.where(first_head, qp, zero), kp) * dmat_ref[2 * p]
            s1 = _dot_nt(jnp.where(first_head, zero, qp), kp) * dmat_ref[2 * p + 1]
            intra = jnp.where(first_head, _dot(s0.astype(BF16), vp), _dot(s1.astype(BF16), vp))
            q32 = qp.astype(F32)
            qd = qdec_ref[p]
            qq = jnp.concatenate([q32 * qd[:, :LANES], q32 * qd[:, LANES:]], axis=1).astype(BF16)
            st = jnp.concatenate([fstate_ref[p], bst_ref[i, p]], axis=0).astype(BF16)
            o = intra + _dot(qq, st)
            o2 = o * o
            s_first = jnp.sum(jnp.where(first_head, o2, 0.0), axis=-1, keepdims=True)
            s_all = jnp.sum(o2, axis=-1, keepdims=True)
            ms = jnp.where(first_head, s_first, s_all - s_first) * (1.0 / dk)
            on = o * lax.rsqrt(ms + EPS) * gn_ref[:, cols]
            g = g_ref[0, rows, cols].astype(F32)
            o_ref[0, rows, cols] = (g * jax.nn.sigmoid(g) * on).astype(BF16)
            upd = kv_outer(kp, kdecf_ref[p], vp)
            fstate_ref[p] = cdecf_ref[p] * fstate_ref[p] + upd
        return carry

    lax.fori_loop(0, nc, fwd_body, 0)


def _retention_tables(p_fwd, p_bwd, C, dk):
    H = p_fwd.shape[0]
    lg_f = -jnp.exp(p_fwd.astype(F32))
    lg_b = -jnp.exp(p_bwd.astype(F32))
    idx = jnp.arange(C, dtype=F32)
    dist = idx[:, None] - idx[None, :]
    dmat = jnp.where(dist >= 0,
                     jnp.exp(lg_f[:, None, None] * jnp.maximum(dist, 0.0)),
                     jnp.exp(lg_b[:, None, None] * jnp.maximum(-dist, 0.0)))

    def lanes(per_head):
        t = jnp.repeat(per_head[:, :, None], dk, axis=2)
        return t.reshape(H // 2, 2, C, dk).transpose(0, 2, 1, 3).reshape(H // 2, C, 2 * dk)

    qdec_f = lanes(jnp.exp(lg_f[:, None] * (idx + 1.0)))
    qdec_b = lanes(jnp.exp(lg_b[:, None] * (C - idx)))
    kdec_f = lanes(jnp.exp(lg_f[:, None] * (C - 1.0 - idx)))
    kdec_b = lanes(jnp.exp(lg_b[:, None] * idx))
    qdec = jnp.concatenate([qdec_f, qdec_b], axis=2)

    def rows(lg):
        per_row = jnp.repeat(jnp.exp(lg * C).reshape(H // 2, 2), dk, axis=1)
        return jnp.broadcast_to(per_row[:, :, None], (H // 2, 2 * dk, 2 * dk))

    return dmat, qdec, kdec_f, kdec_b, rows(lg_f), rows(lg_b)


def _rope_tables(positions, rot_dim, theta, head_dim=64):
    half = rot_dim // 2
    inv_freq = jnp.exp(-math.log(theta) * jnp.arange(half, dtype=F32) * (2.0 / rot_dim))
    ang = positions.astype(F32)[:, :, None] * inv_freq
    cos, sin = jnp.cos(ang), jnp.sin(ang)
    B, L = positions.shape
    pad = head_dim - rot_dim
    cos_h = jnp.concatenate([cos, cos, jnp.ones((B, L, pad), F32)], axis=-1)
    sin_h = jnp.concatenate([-sin, sin, jnp.zeros((B, L, pad), F32)], axis=-1)
    return jnp.concatenate([cos_h, cos_h], axis=-1), jnp.concatenate([sin_h, sin_h], axis=-1)


def retention(proj, cos_t, sin_t, p_fwd, p_bwd, gn_g, *, col0, width, dk=64):
    B, L, _ = proj.shape
    C = min(RET_CHUNK, L)
    assert L % C == 0 and width % LANES == 0
    n_pairs = width // LANES
    nc = L // C
    tables = _retention_tables(p_fwd, p_bwd, C, dk)
    col = lambda c: pl.BlockSpec((1, L, width), lambda b: (b, 0, col0 + c))
    rope = pl.BlockSpec((1, L, LANES), lambda b: (b, 0, 0))
    full = lambda a: pl.BlockSpec(a.shape, lambda b: (0,) * a.ndim)
    gn = gn_g.reshape(1, width).astype(F32)
    return pl.pallas_call(
        functools.partial(_retention_kernel, chunk=C, dk=dk),
        grid=(B,),
        in_specs=[col(0), col(1), col(2), col(3), rope, rope] + [full(t) for t in tables] + [full(gn)],
        out_specs=pl.BlockSpec((1, L, width), lambda b: (b, 0, 0)),
        out_shape=jax.ShapeDtypeStruct((B, L, width), BF16),
        scratch_shapes=[pltpu.VMEM((L, width), BF16), pltpu.VMEM((L, width), BF16),
                        pltpu.VMEM((nc, n_pairs, LANES, LANES), F32),
                        pltpu.VMEM((n_pairs, LANES, LANES), F32),
                        pltpu.VMEM((n_pairs, LANES, LANES), F32)],
        compiler_params=_params("parallel"),
        name="retention",
    )(proj, proj, proj, proj, cos_t, sin_t, *tables, gn)


def _diff_attn_kernel(lam_ref, q_ref, k_ref, v_ref, cosk_ref, sink_ref, cosq_ref, sinq_ref, gn_ref,
                      o_ref, krot_ref, *, d, rot_half, post_scale):
    n_heads = q_ref.shape[2] // LANES

    @pl.when(pl.program_id(1) == 0)
    def _():
        for h in range(n_heads):
            cols = slice(h * LANES, (h + 1) * LANES)
            kr = _rotate_pairs(k_ref[0, :, cols].astype(F32), cosk_ref[0], sink_ref[0], rot_half)
            krot_ref[:, cols] = kr.astype(BF16)

    lam = lam_ref[0, 0]
    tq = q_ref.shape[1]
    first_map = lax.broadcasted_iota(jnp.int32, (tq, LANES), 1) < d

    def softmax(s):
        e = jnp.exp(s - jnp.max(s, axis=-1, keepdims=True))
        return e / jnp.sum(e, axis=-1, keepdims=True)

    for h in range(n_heads):
        cols = slice(h * LANES, (h + 1) * LANES)
        qr = _rotate_pairs(q_ref[0, :, cols].astype(F32), cosq_ref[0], sinq_ref[0], rot_half) * (d ** -0.5)
        kp = krot_ref[:, cols]
        p1 = softmax(_dot_nt(jnp.where(first_map, qr, 0.0).astype(BF16), kp))
        p2 = softmax(_dot_nt(jnp.where(first_map, 0.0, qr).astype(BF16), kp))
        o = _dot((p1 - lam * p2).astype(BF16), v_ref[0, :, cols])
        on = _rms(o, gn_ref[:, cols]) * post_scale
        o_ref[0, :, cols] = on.astype(BF16)


def diff_attention(proj, cos_t, sin_t, lam, gn_g, *, rot_dim, lambda_init, d=64, tq=256):
    B, L, N = proj.shape
    W = N // 3
    tq = min(tq, L)
    assert L % tq == 0
    col = lambda c, rows, f: pl.BlockSpec((1, rows, W), f(c))
    q_map = lambda c: (lambda b, i: (b, i, c))
    kv_map = lambda c: (lambda b, i: (b, 0, c))
    return pl.pallas_call(
        functools.partial(_diff_attn_kernel, d=d, rot_half=rot_dim // 2, post_scale=1.0 - lambda_init),
        grid=(B, L // tq),
        in_specs=[pl.BlockSpec(memory_space=pltpu.SMEM),
                  col(0, tq, q_map), col(1, L, kv_map), col(2, L, kv_map),
                  pl.BlockSpec((1, L, LANES), lambda b, i: (b, 0, 0)),
                  pl.BlockSpec((1, L, LANES), lambda b, i: (b, 0, 0)),
                  pl.BlockSpec((1, tq, LANES), lambda b, i: (b, i, 0)),
                  pl.BlockSpec((1, tq, LANES), lambda b, i: (b, i, 0)),
                  pl.BlockSpec((1, W), lambda b, i: (0, 0))],
        out_specs=pl.BlockSpec((1, tq, W), lambda b, i: (b, i, 0)),
        out_shape=jax.ShapeDtypeStruct((B, L, W), BF16),
        scratch_shapes=[pltpu.VMEM((L, W), BF16)],
        compiler_params=_params("parallel", "arbitrary"),
        name="diff_attention",
    )(lam.reshape(1, 1).astype(F32), proj, proj, proj, cos_t, sin_t, cos_t, sin_t,
      gn_g.reshape(1, W).astype(F32))


def _xattn_kernel(x_ref, g_ref, wq_ref, kv_ref, wo_ref, o_ref, *, heads):
    x = x_ref[0]
    D = x.shape[1]
    hd = D // heads
    hn = _rms(x, g_ref[...]).astype(BF16)
    q = (_dot(hn, wq_ref[...]) * (hd ** -0.5)).astype(BF16)
    outs = []
    for h in range(heads):
        kh = kv_ref[0, :, h * hd:(h + 1) * hd]
        vh = kv_ref[0, :, D + h * hd:D + (h + 1) * hd]
        s = _dot_nt(q[:, h * hd:(h + 1) * hd], kh)
        e = jnp.exp(s - jnp.max(s, axis=-1, keepdims=True))
        oh = _dot(e.astype(BF16), vh) / jnp.sum(e, axis=-1, keepdims=True)
        outs.append(oh.astype(BF16))
    o_ref[0] = x + _dot(jnp.concatenate(outs, axis=-1), wo_ref[...])


def cross_attention(x, g, wq, kv, wo, *, tq=512):
    B, L, D = x.shape
    M = kv.shape[1]
    tq = min(tq, L)
    assert L % tq == 0
    return pl.pallas_call(
        functools.partial(_xattn_kernel, heads=XA_HEADS),
        grid=(B, L // tq),
        in_specs=[pl.BlockSpec((1, tq, D), lambda b, i: (b, i, 0)),
                  pl.BlockSpec((1, D), lambda b, i: (0, 0)),
                  pl.BlockSpec((D, D), lambda b, i: (0, 0)),
                  pl.BlockSpec((1, M, 2 * D), lambda b, i: (b, 0, 0)),
                  pl.BlockSpec((D, D), lambda b, i: (0, 0))],
        out_specs=pl.BlockSpec((1, tq, D), lambda b, i: (b, i, 0)),
        out_shape=jax.ShapeDtypeStruct((B, L, D), F32),
        compiler_params=_params("parallel", "parallel"),
        name="cross_attention",
    )(x, g.reshape(1, D), wq, kv, wo)


def _swiglu_kernel(x_ref, g_ref, wg_ref, wu_ref, wd_ref, o_ref, hn_ref, acc_ref):
    f = pl.program_id(1)

    @pl.when(f == 0)
    def _():
        hn_ref[...] = _rms(x_ref[...], g_ref[...]).astype(BF16)
        acc_ref[...] = jnp.zeros_like(acc_ref)

    hn = hn_ref[...]
    a = _dot(hn, wg_ref[...])
    b = _dot(hn, wu_ref[...])
    acc_ref[...] += _dot((a * jax.nn.sigmoid(a) * b).astype(BF16), wd_ref[...])

    @pl.when(f == pl.num_programs(1) - 1)
    def _():
        o_ref[...] = x_ref[...] + acc_ref[...]


def swiglu_residual(x, g, w_gate, w_up, w_down, *, tm=1024, tf=256):
    T, D = x.shape
    FF = w_gate.shape[1]
    tm, tf = min(tm, T), min(tf, FF)
    assert T % tm == 0 and FF % tf == 0
    return pl.pallas_call(
        _swiglu_kernel,
        grid=(T // tm, FF // tf),
        in_specs=[pl.BlockSpec((tm, D), lambda i, f: (i, 0)),
                  pl.BlockSpec((1, D), lambda i, f: (0, 0)),
                  pl.BlockSpec((D, tf), lambda i, f: (0, f)),
                  pl.BlockSpec((D, tf), lambda i, f: (0, f)),
                  pl.BlockSpec((tf, D), lambda i, f: (f, 0))],
        out_specs=pl.BlockSpec((tm, D), lambda i, f: (i, 0)),
        out_shape=jax.ShapeDtypeStruct((T, D), F32),
        scratch_shapes=[pltpu.VMEM((tm, D), BF16), pltpu.VMEM((tm, D), F32)],
        compiler_params=_params("parallel", "arbitrary"),
        name="swiglu_residual",
    )(x, g.reshape(1, D), w_gate, w_up, w_down)


SLOT_TILE = 512
DISPATCH_CHUNK = 512
COMBINE_TILE = 512
COMBINE_CHUNK = 256
_E1, _E2, _W1, _W2, _RANK1, _RANK2 = range(6)


def _router_kernel(x_ref, g_ref, wr_ref, hn_ref, meta_ref, cum_ref, counts_ref, run_ref, *, n_experts):
    @pl.when(pl.program_id(0) == 0)
    def _():
        run_ref[...] = jnp.zeros_like(run_ref)

    hn = _rms(x_ref[...], g_ref[...])
    hn_ref[...] = hn.astype(BF16)
    logits = jnp.dot(hn, wr_ref[...], preferred_element_type=F32, precision=lax.Precision.HIGHEST)
    tm = logits.shape[0]
    lane = lax.broadcasted_iota(jnp.int32, logits.shape, 1)
    neg = jnp.float32(-jnp.inf)
    logits = jnp.where(lane < n_experts, logits, neg)
    m1 = jnp.max(logits, axis=-1, keepdims=True)
    i1 = jnp.min(jnp.where(logits == m1, lane, LANES), axis=-1, keepdims=True)
    rest = jnp.where(lane == i1, neg, logits)
    m2 = jnp.max(rest, axis=-1, keepdims=True)
    i2 = jnp.min(jnp.where(rest == m2, lane, LANES), axis=-1, keepdims=True)
    e2 = jnp.exp(m2 - m1)
    w1 = 1.0 / (1.0 + e2)
    w2 = e2 / (1.0 + e2)

    chosen = jnp.where((lane == i1) | (lane == i2), 1.0, 0.0)
    r_i = lax.broadcasted_iota(jnp.int32, (tm, tm), 0)
    c_i = lax.broadcasted_iota(jnp.int32, (tm, tm), 1)
    tri = jnp.where(c_i < r_i, 1.0, 0.0).astype(BF16)
    cum = _dot(tri, chosen.astype(BF16)) + run_ref[...]
    cum_ref[...] = cum
    rank1 = jnp.sum(jnp.where(lane == i1, cum, 0.0), axis=-1, keepdims=True)
    rank2 = jnp.sum(jnp.where(lane == i2, cum, 0.0), axis=-1, keepdims=True)
    run_ref[...] += jnp.sum(chosen, axis=0, keepdims=True)
    counts_ref[...] = run_ref[...]

    meta = jnp.zeros_like(logits)
    for idx, val in ((_E1, i1.astype(F32)), (_E2, i2.astype(F32)), (_W1, w1), (_W2, w2),
                     (_RANK1, rank1), (_RANK2, rank2)):
        meta = jnp.where(lane == idx, val, meta)
    meta_ref[...] = meta


def route_tokens(x, g, router_w, *, tm=512):
    T, D = x.shape
    E = router_w.shape[1]
    tm = min(tm, T)
    wr = jnp.zeros((D, LANES), F32).at[:, :E].set(router_w.astype(F32))
    row = pl.BlockSpec((tm, LANES), lambda i: (i, 0))
    return pl.pallas_call(
        functools.partial(_router_kernel, n_experts=E),
        grid=(T // tm,),
        in_specs=[pl.BlockSpec((tm, D), lambda i: (i, 0)),
                  pl.BlockSpec((1, D), lambda i: (0, 0)),
                  pl.BlockSpec((D, LANES), lambda i: (0, 0))],
        out_specs=[pl.BlockSpec((tm, D), lambda i: (i, 0)), row, row,
                   pl.BlockSpec((1, LANES), lambda i: (0, 0))],
        out_shape=[jax.ShapeDtypeStruct((T, D), BF16), jax.ShapeDtypeStruct((T, LANES), F32),
                   jax.ShapeDtypeStruct((T, LANES), F32), jax.ShapeDtypeStruct((1, LANES), F32)],
        scratch_shapes=[pltpu.VMEM((1, LANES), F32)],
        compiler_params=_params("arbitrary"),
        name="route_tokens",
    )(x, g.reshape(1, D), wr)


def _flatten_items(n_items, n_slots):
    i32 = jnp.int32
    end = jnp.cumsum(n_items).astype(i32)
    start = end - n_items
    total = end[-1]
    w = jnp.arange(n_slots, dtype=i32)
    wc = jnp.minimum(w, total - 1)
    owner = jnp.minimum(jnp.sum(wc[:, None] >= end[None, :], axis=1), n_items.shape[0] - 1).astype(i32)
    return owner, wc - start[owner], w < total, wc, start


def _moe_plan(meta, cum, counts, *, n_experts):
    i32 = jnp.int32
    T, E = meta.shape[0], n_experts
    e1, e2 = meta[:, _E1].astype(i32), meta[:, _E2].astype(i32)
    cnt = counts[0, :E].astype(i32)
    tiles_e = (cnt + SLOT_TILE - 1) // SLOT_TILE
    tile_end_e = jnp.cumsum(tiles_e).astype(i32)
    tile_start_e = tile_end_e - tiles_e
    n_used = tile_end_e[-1]
    off_e = tile_start_e * SLOT_TILE
    pos1 = off_e[e1] + meta[:, _RANK1].astype(i32)
    pos2 = off_e[e2] + meta[:, _RANK2].astype(i32)

    n_tiles = 2 * T // SLOT_TILE + E
    r = jnp.arange(n_tiles, dtype=i32)
    e_r = jnp.minimum(jnp.sum(r[:, None] >= tile_end_e[None, :], axis=1), E - 1).astype(i32)
    used_r = r < n_used
    a_r = (r - tile_start_e[e_r]) * SLOT_TILE
    b_r = jnp.minimum(a_r + SLOT_TILE, cnt[e_r])
    cum_incl = jnp.concatenate([cum[1:, :E], counts[:, :E]], axis=0).astype(i32)
    col = cum_incl.T[e_r]
    t_lo = jnp.sum(col <= a_r[:, None], axis=1)
    t_hi = jnp.sum(col <= (b_r - 1)[:, None], axis=1)
    c_lo, c_hi = t_lo // DISPATCH_CHUNK, t_hi // DISPATCH_CHUNK
    n_disp = jnp.where(used_r, c_hi - c_lo + 1, 0).astype(i32)
    owner, k, real, wc, start = _flatten_items(n_disp, E * (T // DISPATCH_CHUNK) + n_tiles)
    disp = (owner, (c_lo[owner] + k).astype(i32),
            jnp.where(real, jnp.where(k == 0, 1, 2), 0).astype(i32))

    n_tok_tiles = T // COMBINE_TILE
    cum_b = cum[::COMBINE_TILE, :E].astype(i32)
    cum_n = jnp.concatenate([cum_b[1:], cnt[None, :]], axis=0)
    s_a, s_b = off_e[None, :] + cum_b, off_e[None, :] + cum_n
    k_lo, k_hi = s_a // COMBINE_CHUNK, (s_b - 1) // COMBINE_CHUNK
    n_comb = jnp.where(s_b > s_a, k_hi - k_lo + 1, 0).astype(i32).reshape(-1)
    n_comb_slots = n_tiles * (SLOT_TILE // COMBINE_CHUNK) + E * n_tok_tiles
    owner, k, real, wc, start = _flatten_items(n_comb, n_comb_slots)
    tok_tile = owner // E
    first = wc == start[tok_tile * E]
    comb = (tok_tile.astype(i32), (k_lo.reshape(-1)[owner] + k).astype(i32),
            jnp.where(real, jnp.where(first, 1, 2), 0).astype(i32))

    record = jnp.zeros((T, LANES), F32)
    record = record.at[:, 0].set(pos1.astype(F32)).at[:, 1].set(pos2.astype(F32))
    record = record.at[:, 2].set(meta[:, _W1]).at[:, 3].set(meta[:, _W2])
    n_chunks = T // DISPATCH_CHUNK
    pos_rows = (pos1.reshape(n_chunks, 1, DISPATCH_CHUNK), pos2.reshape(n_chunks, 1, DISPATCH_CHUNK))
    return dict(pos_rows=pos_rows, record=record, disp=disp, comb=comb, tile_expert=e_r,
                n_used=n_used.reshape(1), n_tiles=n_tiles)


def _dispatch_kernel(tile_ref, chunk_ref, flag_ref, pos1_ref, pos2_ref, hn_ref, o_ref):
    w = pl.program_id(0)
    flag = flag_ref[w]
    rows, cols = o_ref.shape[0], hn_ref.shape[0]

    def gathered():
        slot = tile_ref[w] * rows + lax.broadcasted_iota(jnp.int32, (rows, cols), 0)
        hit = (pos1_ref[0] == slot) | (pos2_ref[0] == slot)
        return _dot(jnp.where(hit, 1.0, 0.0).astype(BF16), hn_ref[...]).astype(BF16)

    @pl.when(flag == 1)
    def _():
        o_ref[...] = gathered()

    @pl.when(flag == 2)
    def _():
        o_ref[...] += gathered()


def dispatch_tokens(hn, plan):
    T, D = hn.shape
    tile, chunk, flag = plan["disp"]
    pos1, pos2 = plan["pos_rows"]
    pos_spec = pl.BlockSpec((1, 1, DISPATCH_CHUNK), lambda w, t, c, f: (c[w], 0, 0))
    return pl.pallas_call(
        _dispatch_kernel,
        grid_spec=pltpu.PrefetchScalarGridSpec(
            num_scalar_prefetch=3,
            grid=(tile.shape[0],),
            in_specs=[pos_spec, pos_spec,
                      pl.BlockSpec((DISPATCH_CHUNK, D), lambda w, t, c, f: (c[w], 0))],
            out_specs=pl.BlockSpec((SLOT_TILE, D), lambda w, t, c, f: (t[w], 0)),
        ),
        out_shape=jax.ShapeDtypeStruct((plan["n_tiles"] * SLOT_TILE, D), BF16),
        compiler_params=_params("arbitrary"),
        name="moe_dispatch",
    )(tile, chunk, flag, pos1, pos2, hn)


def _expert_kernel(te_ref, nu_ref, x_ref, wg_ref, wu_ref, wd_ref, o_ref, acc_ref):
    r, f = pl.program_id(0), pl.program_id(1)

    @pl.when(f == 0)
    def _():
        acc_ref[...] = jnp.zeros_like(acc_ref)

    @pl.when(r < nu_ref[0])
    def _():
        x = x_ref[...]
        a = _dot(x, wg_ref[0])
        b = _dot(x, wu_ref[0])
        acc_ref[...] += _dot((a * jax.nn.sigmoid(a) * b).astype(BF16), wd_ref[0])

    @pl.when(f == pl.num_programs(1) - 1)
    def _():
        o_ref[...] = acc_ref[...].astype(BF16)


def expert_swiglu(xs, plan, w_gate, w_up, w_down, *, tf=512):
    S, D = xs.shape
    FF = w_gate.shape[2]
    tf = min(tf, FF)
    assert FF % tf == 0
    nf = FF // tf
    last = lambda r, nu: jnp.minimum(r, nu[0] - 1)
    col = lambda r, f, nu: jnp.where(r < nu[0], f, nf - 1)
    return pl.pallas_call(
        _expert_kernel,
        grid_spec=pltpu.PrefetchScalarGridSpec(
            num_scalar_prefetch=2,
            grid=(S // SLOT_TILE, nf),
            in_specs=[pl.BlockSpec((SLOT_TILE, D), lambda r, f, te, nu: (last(r, nu), 0)),
                      pl.BlockSpec((1, D, tf), lambda r, f, te, nu: (te[last(r, nu)], 0, col(r, f, nu))),
                      pl.BlockSpec((1, D, tf), lambda r, f, te, nu: (te[last(r, nu)], 0, col(r, f, nu))),
                      pl.BlockSpec((1, tf, D), lambda r, f, te, nu: (te[last(r, nu)], col(r, f, nu), 0))],
            out_specs=pl.BlockSpec((SLOT_TILE, D), lambda r, f, te, nu: (r, 0)),
            scratch_shapes=[pltpu.VMEM((SLOT_TILE, D), F32)],
        ),
        out_shape=jax.ShapeDtypeStruct((S, D), BF16),
        compiler_params=_params("arbitrary", "arbitrary"),
        name="moe_experts",
    )(plan["tile_expert"], plan["n_used"], xs, w_gate, w_up, w_down)


def _combine_kernel(tile_ref, chunk_ref, flag_ref, rec_ref, y_ref, x_ref, o_ref):
    w = pl.program_id(0)
    flag = flag_ref[w]
    rows, cols = o_ref.shape[0], y_ref.shape[0]

    def weighted():
        rec = rec_ref[...]
        slot = (chunk_ref[w] * cols + lax.broadcasted_iota(jnp.int32, (rows, cols), 1)).astype(F32)
        sel = jnp.where(rec[:, 0:1] == slot, rec[:, 2:3], 0.0) + jnp.where(rec[:, 1:2] == slot, rec[:, 3:4], 0.0)
        return _dot(sel.astype(BF16), y_ref[...])

    @pl.when(flag == 1)
    def _():
        o_ref[...] = x_ref[...] + weighted()

    @pl.when(flag == 2)
    def _():
        o_ref[...] += weighted()


def combine_tokens(x, y, plan):
    T, D = x.shape
    tile, chunk, flag = plan["comb"]
    return pl.pallas_call(
        _combine_kernel,
        grid_spec=pltpu.PrefetchScalarGridSpec(
            num_scalar_prefetch=3,
            grid=(tile.shape[0],),
            in_specs=[pl.BlockSpec((COMBINE_TILE, LANES), lambda w, t, c, f: (t[w], 0)),
                      pl.BlockSpec((COMBINE_CHUNK, D), lambda w, t, c, f: (c[w], 0)),
                      pl.BlockSpec((COMBINE_TILE, D), lambda w, t, c, f: (t[w], 0))],
            out_specs=pl.BlockSpec((COMBINE_TILE, D), lambda w, t, c, f: (t[w], 0)),
        ),
        out_shape=jax.ShapeDtypeStruct((T, D), F32),
        compiler_params=_params("arbitrary"),
        name="moe_combine",
    )(tile, chunk, flag, plan["record"], y, x)


def moe_residual(x, g, router_w, w_gate, w_up, w_down):
    hn, meta, cum, counts = route_tokens(x, g, router_w)
    plan = _moe_plan(meta, cum, counts, n_experts=router_w.shape[1])
    xs = dispatch_tokens(hn, plan)
    y = expert_swiglu(xs, plan, w_gate, w_up, w_down)
    return combine_tokens(x, y, plan)


def _final_norm_kernel(x_ref, g_ref, o_ref):
    o_ref[...] = _rms(x_ref[...], g_ref[...])


def final_norm(x, g, *, tm=1024):
    T, D = x.shape
    tm = min(tm, T)
    return pl.pallas_call(
        _final_norm_kernel,
        grid=(T // tm,),
        in_specs=[pl.BlockSpec((tm, D), lambda i: (i, 0)), pl.BlockSpec((1, D), lambda i: (0, 0))],
        out_specs=pl.BlockSpec((tm, D), lambda i: (i, 0)),
        out_shape=jax.ShapeDtypeStruct((T, D), F32),
        compiler_params=_params("parallel"),
        name="final_norm",
    )(x, g.reshape(1, D))


def kernel(x, mem, positions, mem_norm_g, final_norm_g, mix_norm_g, xa_norm_g, ffn_norm_g, xa_wq, xa_wkv, xa_wo, ev_w_in, ev_conv_w, ev_ret_decay_f, ev_ret_decay_b, ev_ret_gn_g, ev_w_out, ffd_w_gate, ffd_w_up, ffd_w_down, od_w_in, od_lam_q1, od_lam_k1, od_lam_q2, od_lam_k2, od_gn_g, od_w_out, moe_router, moe_w_gate, moe_w_up, moe_w_down):
    B, L, D = x.shape
    M = mem.shape[1]
    T = B * L
    depth = mix_norm_g.shape[0]
    conv_ch = ev_conv_w.shape[2]
    ret_w = ev_ret_gn_g.shape[1]
    diff_d = od_lam_q1.shape[1]
    bf = lambda w: w.astype(BF16)

    ret_dk = ret_w // RET_HEADS
    ret_cos, ret_sin = _rope_tables(positions, ret_dk, RET_THETA)
    diff_cos, diff_sin = _rope_tables(positions, diff_d // 4, ROPE_THETA)
    mem2 = mem.reshape(B * M, D)
    xt = x.reshape(T, D)

    for i in range(depth):
        j = i // 2
        if i % 2 == 0:
            proj = norm_proj(xt, mix_norm_g[i], bf(ev_w_in[j])).reshape(B, L, -1)
            a = short_conv(proj, ev_conv_w[j].astype(F32), ch=conv_ch)
            r = retention(proj, ret_cos, ret_sin, ev_ret_decay_f[j], ev_ret_decay_b[j], ev_ret_gn_g[j],
                          col0=3 * conv_ch // ret_w, width=ret_w, dk=ret_dk)
            w_out = bf(ev_w_out[j])
            xt = proj_residual([a.reshape(T, conv_ch), r.reshape(T, ret_w)],
                               [w_out[:conv_ch], w_out[conv_ch:]], xt)
        else:
            lambda_init = 0.8 - 0.6 * math.exp(-0.3 * i)
            f32 = lambda v: v.astype(F32)
            lam = (jnp.exp(jnp.sum(f32(od_lam_q1[j]) * f32(od_lam_k1[j])))
                   - jnp.exp(jnp.sum(f32(od_lam_q2[j]) * f32(od_lam_k2[j]))) + lambda_init)
            proj = norm_proj(xt, mix_norm_g[i], bf(od_w_in[j])).reshape(B, L, -1)
            o = diff_attention(proj, diff_cos, diff_sin, lam, od_gn_g[j], rot_dim=diff_d // 4,
                               lambda_init=lambda_init, d=diff_d)
            xt = proj_residual([o.reshape(T, -1)], [bf(od_w_out[j])], xt)

        kv = norm_proj(mem2, mem_norm_g, bf(xa_wkv[i])).reshape(B, M, 2 * D)
        xt = cross_attention(xt.reshape(B, L, D), xa_norm_g[i], bf(xa_wq[i]), kv, bf(xa_wo[i])).reshape(T, D)

        if i % 2 == 0:
            xt = swiglu_residual(xt, ffn_norm_g[i], bf(ffd_w_gate[j]), bf(ffd_w_up[j]), bf(ffd_w_down[j]))
        else:
            xt = moe_residual(xt, ffn_norm_g[i], moe_router[j], bf(moe_w_gate[j]), bf(moe_w_up[j]),
                              bf(moe_w_down[j]))

    return final_norm(xt, final_norm_g).reshape(B, L, D)
```

```python
import functools
import math

import jax
import jax.numpy as jnp
from jax import lax
from jax.experimental import pallas as pl
from jax.experimental.pallas import tpu as pltpu

F32 = jnp.float32
BF16 = jnp.bfloat16
EPS = 1e-6
LANES = 128
LOG2E = 1.4426950408889634

RET_HEADS = 8
RET_THETA = 10000.0
RET_CHUNK = 256
ROPE_THETA = 500000.0
XA_HEADS = 4

VMEM_LIMIT = 56 * 1024 * 1024


def _params(*sem):
    return pltpu.CompilerParams(dimension_semantics=sem, vmem_limit_bytes=VMEM_LIMIT)


def _rms(x, g):
    ms = jnp.mean(x * x, axis=-1, keepdims=True)
    return x * lax.rsqrt(ms + EPS) * g


def _dot(a, b):
    return jnp.dot(a, b, preferred_element_type=F32)


def _dot_nt(a, b):
    return lax.dot_general(a, b, (((1,), (1,)), ((), ())), preferred_element_type=F32)


def _rotate_pairs(x, cos_t, sin_t, half):
    lane = lax.broadcasted_iota(jnp.int32, x.shape, 1) % 64
    partner = jnp.where(lane < half, pltpu.roll(x, LANES - half, axis=1), pltpu.roll(x, half, axis=1))
    return x * cos_t + partner * sin_t


def _layer_spec(w, layer):
    return pl.BlockSpec((1,) + w.shape[1:], lambda *_: (layer, 0, 0))


def _norm_proj_kernel(x_ref, g_ref, w_ref, o_ref, *, tn):
    hn = _rms(x_ref[...], g_ref[...]).astype(BF16)
    for c in range(o_ref.shape[1] // tn):
        cols = slice(c * tn, (c + 1) * tn)
        o_ref[:, cols] = _dot(hn, w_ref[0, :, cols]).astype(o_ref.dtype)


def norm_proj(x, g, w, layer, *, tm=1024, tn=512):
    T, D = x.shape
    N = w.shape[2]
    tm, tn = min(tm, T), min(tn, N)
    assert T % tm == 0 and N % tn == 0
    return pl.pallas_call(
        functools.partial(_norm_proj_kernel, tn=tn),
        grid=(T // tm,),
        in_specs=[pl.BlockSpec((tm, D), lambda i: (i, 0)),
                  pl.BlockSpec((1, D), lambda i: (0, 0)),
                  _layer_spec(w, layer)],
        out_specs=pl.BlockSpec((tm, N), lambda i: (i, 0)),
        out_shape=jax.ShapeDtypeStruct((T, N), BF16),
        compiler_params=_params("parallel"),
        name="norm_proj",
    )(x, g.reshape(1, D), w)


def _conv_kernel(gb_ref, gc_ref, h_ref, w_ref, o_ref):
    u = gc_ref[0].astype(F32) * h_ref[0].astype(F32)
    L = u.shape[0]
    row = lax.broadcasted_iota(jnp.int32, u.shape, 0)
    prev = jnp.where(row == 0, 0.0, pltpu.roll(u, 1, axis=0))
    nxt = jnp.where(row == L - 1, 0.0, pltpu.roll(u, L - 1, axis=0))
    conv = w_ref[0, 0:1, :] * prev + w_ref[0, 1:2, :] * u + w_ref[0, 2:3, :] * nxt
    o_ref[0] = (gb_ref[0].astype(F32) * conv).astype(BF16)


def short_conv(proj, conv_w, layer, *, ch):
    B, L, _ = proj.shape
    col = lambda c: pl.BlockSpec((1, L, ch), lambda b: (b, 0, c))
    return pl.pallas_call(
        _conv_kernel,
        grid=(B,),
        in_specs=[col(0), col(1), col(2), _layer_spec(conv_w, layer)],
        out_specs=pl.BlockSpec((1, L, ch), lambda b: (b, 0, 0)),
        out_shape=jax.ShapeDtypeStruct((B, L, ch), BF16),
        compiler_params=_params("parallel"),
        name="short_conv",
    )(proj, proj, proj, conv_w)


def _retention_kernel(q_ref, k_ref, v_ref, g_ref, cos_ref, sin_ref, dmat_ref, qdec_ref, kdecf_ref,
                      kdecb_ref, cdecf_ref, cdecb_ref, gn_ref, o_ref, qrot_ref, krot_ref, bst_ref,
                      fstate_ref, bstate_ref, *, chunk, dk):
    L = q_ref.shape[1]
    n_pairs = q_ref.shape[2] // LANES
    nc = L // chunk
    C = chunk
    cos_t, sin_t = cos_ref[0], sin_ref[0]

    for p in range(n_pairs):
        cols = slice(p * LANES, (p + 1) * LANES)
        qrot_ref[:, cols] = _rotate_pairs(q_ref[0, :, cols].astype(F32), cos_t, sin_t, dk // 2).astype(BF16)
        kr = _rotate_pairs(k_ref[0, :, cols].astype(F32), cos_t, sin_t, dk // 2)
        krot_ref[:, cols] = (kr * (dk ** -0.5)).astype(BF16)

    lane_c = lax.broadcasted_iota(jnp.int32, (C, LANES), 1)
    first_head = lane_c < dk
    r_i = lax.broadcasted_iota(jnp.int32, (LANES, LANES), 0) // dk
    c_i = lax.broadcasted_iota(jnp.int32, (LANES, LANES), 1) // dk
    same_head = r_i == c_i

    def kv_outer(kp, kdec, vp):
        kd_t = (kp.astype(F32) * kdec).T.astype(BF16)
        return jnp.where(same_head, _dot(kd_t, vp), 0.0)

    bstate_ref[...] = jnp.zeros_like(bstate_ref)

    def back_body(t, carry):
        i = nc - 1 - t
        rows = pl.ds(pl.multiple_of(i * C, C), C)
        for p in range(n_pairs):
            cols = slice(p * LANES, (p + 1) * LANES)
            bst_ref[i, p] = bstate_ref[p]
            upd = kv_outer(krot_ref[rows, cols], kdecb_ref[p], v_ref[0, rows, cols])
            bstate_ref[p] = cdecb_ref[p] * bstate_ref[p] + upd
        return carry

    lax.fori_loop(0, nc, back_body, 0)

    fstate_ref[...] = jnp.zeros_like(fstate_ref)

    def fwd_body(i, carry):
        rows = pl.ds(pl.multiple_of(i * C, C), C)
        for p in range(n_pairs):
            cols = slice(p * LANES, (p + 1) * LANES)
            qp, kp, vp = qrot_ref[rows, cols], krot_ref[rows, cols], v_ref[0, rows, cols]
            zero = jnp.zeros_like(qp)
            s0 = _dot_nt(jnp.where(first_head, qp, zero), kp) * dmat_ref[2 * p]
            s1 = _dot_nt(jnp.where(first_head, zero, qp), kp) * dmat_ref[2 * p + 1]
            intra = jnp.where(first_head, _dot(s0.astype(BF16), vp), _dot(s1.astype(BF16), vp))
            q32 = qp.astype(F32)
            qd = qdec_ref[p]
            qq = jnp.concatenate([q32 * qd[:, :LANES], q32 * qd[:, LANES:]], axis=1).astype(BF16)
            st = jnp.concatenate([fstate_ref[p], bst_ref[i, p]], axis=0).astype(BF16)
            o = intra + _dot(qq, st)
            o2 = o * o
            s_first = jnp.sum(jnp.where(first_head, o2, 0.0), axis=-1, keepdims=True)
            s_all = jnp.sum(o2, axis=-1, keepdims=True)
            ms = jnp.where(first_head, s_first, s_all - s_first) * (1.0 / dk)
            on = o * lax.rsqrt(ms + EPS) * gn_ref[:, cols]
            g = g_ref[0, rows, cols].astype(F32)
            o_ref[0, rows, cols] = (g * jax.nn.sigmoid(g) * on).astype(BF16)
            upd = kv_outer(kp, kdecf_ref[p], vp)
            fstate_ref[p] = cdecf_ref[p] * fstate_ref[p] + upd
        return carry

    lax.fori_loop(0, nc, fwd_body, 0)


def _retention_tables(p_fwd, p_bwd, C, dk):
    H = p_fwd.shape[0]
    lg_f = -jnp.exp(p_fwd.astype(F32))
    lg_b = -jnp.exp(p_bwd.astype(F32))
    idx = jnp.arange(C, dtype=F32)
    dist = idx[:, None] - idx[None, :]
    dmat = jnp.where(dist >= 0,
                     jnp.exp(lg_f[:, None, None] * jnp.maximum(dist, 0.0)),
                     jnp.exp(lg_b[:, None, None] * jnp.maximum(-dist, 0.0)))

    def lanes(per_head):
        t = jnp.repeat(per_head[:, :, None], dk, axis=2)
        return t.reshape(H // 2, 2, C, dk).transpose(0, 2, 1, 3).reshape(H // 2, C, 2 * dk)

    qdec_f = lanes(jnp.exp(lg_f[:, None] * (idx + 1.0)))
    qdec_b = lanes(jnp.exp(lg_b[:, None] * (C - idx)))
    kdec_f = lanes(jnp.exp(lg_f[:, None] * (C - 1.0 - idx)))
    kdec_b = lanes(jnp.exp(lg_b[:, None] * idx))
    qdec = jnp.concatenate([qdec_f, qdec_b], axis=2)

    def rows(lg):
        per_row = jnp.repeat(jnp.exp(lg * C).reshape(H // 2, 2), dk, axis=1)
        return jnp.broadcast_to(per_row[:, :, None], (H // 2, 2 * dk, 2 * dk))

    return dmat, qdec, kdec_f, kdec_b, rows(lg_f), rows(lg_b)


def _rope_tables(positions, rot_dim, theta, head_dim=64):
    half = rot_dim // 2
    j = jnp.arange(LANES) % head_dim
    inv_freq = jnp.exp(-math.log(theta) * (j % half).astype(F32) * (2.0 / rot_dim))
    ang = positions.astype(F32)[:, :, None] * inv_freq
    cos_t = jnp.where(j < rot_dim, jnp.cos(ang), 1.0)
    sin_t = jnp.where(j < rot_dim, jnp.where(j < half, -jnp.sin(ang), jnp.sin(ang)), 0.0)
    return cos_t, sin_t


def retention(proj, cos_t, sin_t, p_fwd, p_bwd, gn_g, *, col0, width, dk=64):
    B, L, _ = proj.shape
    C = min(RET_CHUNK, L)
    assert L % C == 0 and width % LANES == 0
    n_pairs = width // LANES
    nc = L // C
    tables = _retention_tables(p_fwd, p_bwd, C, dk)
    col = lambda c: pl.BlockSpec((1, L, width), lambda b: (b, 0, col0 + c))
    rope = pl.BlockSpec((1, L, LANES), lambda b: (b, 0, 0))
    full = lambda a: pl.BlockSpec(a.shape, lambda b: (0,) * a.ndim)
    gn = gn_g.reshape(1, width).astype(F32)
    return pl.pallas_call(
        functools.partial(_retention_kernel, chunk=C, dk=dk),
        grid=(B,),
        in_specs=[col(0), col(1), col(2), col(3), rope, rope] + [full(t) for t in tables] + [full(gn)],
        out_specs=pl.BlockSpec((1, L, width), lambda b: (b, 0, 0)),
        out_shape=jax.ShapeDtypeStruct((B, L, width), BF16),
        scratch_shapes=[pltpu.VMEM((L, width), BF16), pltpu.VMEM((L, width), BF16),
                        pltpu.VMEM((nc, n_pairs, LANES, LANES), F32),
                        pltpu.VMEM((n_pairs, LANES, LANES), F32),
                        pltpu.VMEM((n_pairs, LANES, LANES), F32)],
        compiler_params=_params("parallel"),
        name="retention",
    )(proj, proj, proj, proj, cos_t, sin_t, *tables, gn)


def _diff_attn_kernel(lam_ref, q_ref, k_ref, v_ref, cosk_ref, sink_ref, cosq_ref, sinq_ref, gn_ref,
                      o_ref, krot_ref, *, d, rot_half, post_scale):
    n_heads = q_ref.shape[2] // LANES

    @pl.when(pl.program_id(1) == 0)
    def _():
        for h in range(n_heads):
            cols = slice(h * LANES, (h + 1) * LANES)
            kr = _rotate_pairs(k_ref[0, :, cols].astype(F32), cosk_ref[0], sink_ref[0], rot_half)
            krot_ref[:, cols] = kr.astype(BF16)

    lam = lam_ref[0, 0]
    tq = q_ref.shape[1]
    first_map = lax.broadcasted_iota(jnp.int32, (tq, LANES), 1) < d

    def attend(q, kp, vp):
        s = _dot_nt(q.astype(BF16), kp)
        e = jnp.exp2(s - jnp.max(s, axis=-1, keepdims=True))
        return _dot(e.astype(BF16), vp) / jnp.sum(e, axis=-1, keepdims=True)

    for h in range(n_heads):
        cols = slice(h * LANES, (h + 1) * LANES)
        qr = _rotate_pairs(q_ref[0, :, cols].astype(F32), cosq_ref[0], sinq_ref[0], rot_half)
        qr = qr * (d ** -0.5 * LOG2E)
        kp, vp = krot_ref[:, cols], v_ref[0, :, cols]
        o = attend(jnp.where(first_map, qr, 0.0), kp, vp) - lam * attend(jnp.where(first_map, 0.0, qr), kp, vp)
        on = _rms(o, gn_ref[:, cols]) * post_scale
        o_ref[0, :, cols] = on.astype(BF16)


def diff_attention(proj, cos_t, sin_t, lam, gn_g, *, rot_dim, lambda_init, d=64, tq=256):
    B, L, N = proj.shape
    W = N // 3
    tq = min(tq, L)
    assert L % tq == 0
    col = lambda c, rows, f: pl.BlockSpec((1, rows, W), f(c))
    q_map = lambda c: (lambda b, i: (b, i, c))
    kv_map = lambda c: (lambda b, i: (b, 0, c))
    return pl.pallas_call(
        functools.partial(_diff_attn_kernel, d=d, rot_half=rot_dim // 2, post_scale=1.0 - lambda_init),
        grid=(B, L // tq),
        in_specs=[pl.BlockSpec(memory_space=pltpu.SMEM),
                  col(0, tq, q_map), col(1, L, kv_map), col(2, L, kv_map),
                  pl.BlockSpec((1, L, LANES), lambda b, i: (b, 0, 0)),
                  pl.BlockSpec((1, L, LANES), lambda b, i: (b, 0, 0)),
                  pl.BlockSpec((1, tq, LANES), lambda b, i: (b, i, 0)),
                  pl.BlockSpec((1, tq, LANES), lambda b, i: (b, i, 0)),
                  pl.BlockSpec((1, W), lambda b, i: (0, 0))],
        out_specs=pl.BlockSpec((1, tq, W), lambda b, i: (b, i, 0)),
        out_shape=jax.ShapeDtypeStruct((B, L, W), BF16),
        scratch_shapes=[pltpu.VMEM((L, W), BF16)],
        compiler_params=_params("parallel", "arbitrary"),
        name="diff_attention",
    )(lam.reshape(1, 1).astype(F32), proj, proj, proj, cos_t, sin_t, cos_t, sin_t,
      gn_g.reshape(1, W).astype(F32))


def _mix_xattn_kernel(*refs, n_parts, heads):
    parts, w_out_ref = refs[:n_parts], refs[n_parts]
    x_ref, g_ref, wq_ref, kv_ref, wo_ref, o_ref = refs[n_parts + 1:]
    x = x_ref[0]
    D = x.shape[1]
    hd = D // heads
    row = 0
    for a_ref in parts:
        k = a_ref.shape[2]
        x = x + _dot(a_ref[0], w_out_ref[0, row:row + k, :])
        row += k
    hn = _rms(x, g_ref[...]).astype(BF16)
    q = (_dot(hn, wq_ref[0]) * (hd ** -0.5)).astype(BF16)
    outs = []
    for h in range(heads):
        kh = kv_ref[0, :, h * hd:(h + 1) * hd]
        vh = kv_ref[0, :, D + h * hd:D + (h + 1) * hd]
        s = _dot_nt(q[:, h * hd:(h + 1) * hd], kh)
        e = jnp.exp(s - jnp.max(s, axis=-1, keepdims=True))
        oh = _dot(e.astype(BF16), vh) / jnp.sum(e, axis=-1, keepdims=True)
        outs.append(oh.astype(BF16))
    o_ref[0] = x + _dot(jnp.concatenate(outs, axis=-1), wo_ref[0])


def mixer_out_cross_attention(x, parts, w_out, mix_layer, g, wq, kv, wo, layer, *, tq=512):
    B, L, D = x.shape
    M = kv.shape[1]
    tq = min(tq, L)
    assert L % tq == 0
    tile = lambda width: pl.BlockSpec((1, tq, width), lambda b, i: (b, i, 0))
    return pl.pallas_call(
        functools.partial(_mix_xattn_kernel, n_parts=len(parts), heads=XA_HEADS),
        grid=(B, L // tq),
        in_specs=[tile(a.shape[2]) for a in parts] + [
            _layer_spec(w_out, mix_layer), tile(D), pl.BlockSpec((1, D), lambda b, i: (0, 0)),
            _layer_spec(wq, layer), pl.BlockSpec((1, M, 2 * D), lambda b, i: (b, 0, 0)),
            _layer_spec(wo, layer)],
        out_specs=tile(D),
        out_shape=jax.ShapeDtypeStruct((B, L, D), F32),
        compiler_params=_params("parallel", "parallel"),
        name="mixer_out_cross_attention",
    )(*parts, w_out, x, g.reshape(1, D), wq, kv, wo)


def _swiglu_kernel(x_ref, g_ref, wg_ref, wu_ref, wd_ref, o_ref, *, tf):
    x = x_ref[...]
    hn = _rms(x, g_ref[...]).astype(BF16)
    acc = x
    for c in range(wg_ref.shape[2] // tf):
        cols = slice(c * tf, (c + 1) * tf)
        a = _dot(hn, wg_ref[0, :, cols])
        b = _dot(hn, wu_ref[0, :, cols])
        acc = acc + _dot((a * jax.nn.sigmoid(a) * b).astype(BF16), wd_ref[0, cols, :])
    o_ref[...] = acc


def swiglu_residual(x, g, w_gate, w_up, w_down, layer, *, tm=512, tf=256):
    T, D = x.shape
    FF = w_gate.shape[2]
    tm, tf = min(tm, T), min(tf, FF)
    assert T % tm == 0 and FF % tf == 0
    return pl.pallas_call(
        functools.partial(_swiglu_kernel, tf=tf),
        grid=(T // tm,),
        in_specs=[pl.BlockSpec((tm, D), lambda i: (i, 0)),
                  pl.BlockSpec((1, D), lambda i: (0, 0)),
                  _layer_spec(w_gate, layer), _layer_spec(w_up, layer), _layer_spec(w_down, layer)],
        out_specs=pl.BlockSpec((tm, D), lambda i: (i, 0)),
        out_shape=jax.ShapeDtypeStruct((T, D), F32),
        compiler_params=_params("parallel"),
        name="swiglu_residual",
    )(x, g.reshape(1, D), w_gate, w_up, w_down)


SLOT_TILE = 1024
DISPATCH_TILE = 512
DISPATCH_CHUNK = 512
COMBINE_TILE = 512
COMBINE_CHUNK = 256
ROUTE_ROWS = 8
_E1, _E2, _W1, _W2, _RANK1, _RANK2 = range(6)
_POS1, _POS2, _GATE1, _GATE2 = range(4)


def _router_kernel(x_ref, g_ref, wr_ref, hn_ref, meta_ref, cum_ref, counts_ref, run_ref, *, n_experts):
    @pl.when(pl.program_id(0) == 0)
    def _():
        run_ref[...] = jnp.zeros_like(run_ref)

    hn = _rms(x_ref[...], g_ref[...])
    hn_ref[...] = hn.astype(BF16)
    logits = jnp.dot(hn, wr_ref[...], preferred_element_type=F32, precision=lax.Precision.HIGHEST)
    tm = logits.shape[0]
    lane = lax.broadcasted_iota(jnp.int32, logits.shape, 1)
    neg = jnp.float32(-jnp.inf)
    logits = jnp.where(lane < n_experts, logits, neg)
    m1 = jnp.max(logits, axis=-1, keepdims=True)
    i1 = jnp.min(jnp.where(logits == m1, lane, LANES), axis=-1, keepdims=True)
    rest = jnp.where(lane == i1, neg, logits)
    m2 = jnp.max(rest, axis=-1, keepdims=True)
    i2 = jnp.min(jnp.where(rest == m2, lane, LANES), axis=-1, keepdims=True)
    e2 = jnp.exp(m2 - m1)
    w1 = 1.0 / (1.0 + e2)
    w2 = e2 / (1.0 + e2)

    chosen = jnp.where((lane == i1) | (lane == i2), 1.0, 0.0)
    r_i = lax.broadcasted_iota(jnp.int32, (tm, tm), 0)
    c_i = lax.broadcasted_iota(jnp.int32, (tm, tm), 1)
    tri = jnp.where(c_i < r_i, 1.0, 0.0).astype(BF16)
    cum = _dot(tri, chosen.astype(BF16)) + run_ref[...]
    cum_ref[...] = cum
    rank1 = jnp.sum(jnp.where(lane == i1, cum, 0.0), axis=-1, keepdims=True)
    rank2 = jnp.sum(jnp.where(lane == i2, cum, 0.0), axis=-1, keepdims=True)
    run_ref[...] += jnp.sum(chosen, axis=0, keepdims=True)
    counts_ref[...] = run_ref[...]

    meta = jnp.zeros_like(logits)
    for idx, val in ((_E1, i1.astype(F32)), (_E2, i2.astype(F32)), (_W1, w1), (_W2, w2),
                     (_RANK1, rank1), (_RANK2, rank2)):
        meta = jnp.where(lane == idx, val, meta)
    meta_ref[...] = meta


def route_tokens(x, g, router_w, *, tm=512):
    T, D = x.shape
    E = router_w.shape[1]
    tm = min(tm, T)
    wr = jnp.pad(router_w.astype(F32), ((0, 0), (0, LANES - E)))
    row = pl.BlockSpec((tm, LANES), lambda i: (i, 0))
    return pl.pallas_call(
        functools.partial(_router_kernel, n_experts=E),
        grid=(T // tm,),
        in_specs=[pl.BlockSpec((tm, D), lambda i: (i, 0)),
                  pl.BlockSpec((1, D), lambda i: (0, 0)),
                  pl.BlockSpec((D, LANES), lambda i: (0, 0))],
        out_specs=[pl.BlockSpec((tm, D), lambda i: (i, 0)), row, row,
                   pl.BlockSpec((1, LANES), lambda i: (0, 0))],
        out_shape=[jax.ShapeDtypeStruct((T, D), BF16), jax.ShapeDtypeStruct((T, LANES), F32),
                   jax.ShapeDtypeStruct((T, LANES), F32), jax.ShapeDtypeStruct((1, LANES), F32)],
        scratch_shapes=[pltpu.VMEM((1, LANES), F32)],
        compiler_params=_params("arbitrary"),
        name="route_tokens",
    )(x, g.reshape(1, D), wr)


def _slots_kernel(off_ref, meta_ref, rec_ref, rows_ref, *, n_experts):
    meta = meta_ref[...]
    e1, e2 = meta[:, _E1:_E1 + 1], meta[:, _E2:_E2 + 1]
    off1, off2 = jnp.zeros_like(e1), jnp.zeros_like(e2)
    for e in range(n_experts):
        off = off_ref[e].astype(F32)
        off1 = jnp.where(e1 == e, off, off1)
        off2 = jnp.where(e2 == e, off, off2)
    lane = lax.broadcasted_iota(jnp.int32, meta.shape, 1)
    rec = jnp.zeros_like(meta)
    for idx, val in ((_POS1, off1 + meta[:, _RANK1:_RANK1 + 1]), (_POS2, off2 + meta[:, _RANK2:_RANK2 + 1]),
                     (_GATE1, meta[:, _W1:_W1 + 1]), (_GATE2, meta[:, _W2:_W2 + 1])):
        rec = jnp.where(lane == idx, val, rec)
    rec_ref[...] = rec
    rows_ref[...] = rec.T[:ROUTE_ROWS, :]


def assign_slots(meta, group_offset, *, n_experts, tm=512):
    T = meta.shape[0]
    tm = min(tm, T)
    return pl.pallas_call(
        functools.partial(_slots_kernel, n_experts=n_experts),
        grid_spec=pltpu.PrefetchScalarGridSpec(
            num_scalar_prefetch=1,
            grid=(T // tm,),
            in_specs=[pl.BlockSpec((tm, LANES), lambda i, off: (i, 0))],
            out_specs=[pl.BlockSpec((tm, LANES), lambda i, off: (i, 0)),
                       pl.BlockSpec((ROUTE_ROWS, tm), lambda i, off: (0, i))],
        ),
        out_shape=[jax.ShapeDtypeStruct((T, LANES), F32), jax.ShapeDtypeStruct((ROUTE_ROWS, T), F32)],
        compiler_params=_params("parallel"),
        name="assign_slots",
    )(group_offset, meta)


def _flatten_items(n_items, n_slots):
    i32 = jnp.int32
    end = jnp.cumsum(n_items).astype(i32)
    start = end - n_items
    total = end[-1]
    w = jnp.arange(n_slots, dtype=i32)
    wc = jnp.minimum(w, total - 1)
    owner = jnp.minimum(jnp.sum(wc[:, None] >= end[None, :], axis=1), n_items.shape[0] - 1).astype(i32)
    return owner, wc - start[owner], w < total, wc, start


def _moe_plan(cum, counts, *, n_experts, expert_base):
    i32 = jnp.int32
    T, E = cum.shape[0], n_experts
    cnt = counts[0, :E].astype(i32)
    tiles_e = (cnt + SLOT_TILE - 1) // SLOT_TILE
    tile_end_e = jnp.cumsum(tiles_e).astype(i32)
    tile_start_e = tile_end_e - tiles_e
    n_used = tile_end_e[-1]
    off_e = tile_start_e * SLOT_TILE

    n_tiles = 2 * T // SLOT_TILE + E
    r = jnp.arange(n_tiles, dtype=i32)
    tile_expert = jnp.minimum(jnp.sum(r[:, None] >= tile_end_e[None, :], axis=1), E - 1).astype(i32)

    per = SLOT_TILE // DISPATCH_TILE
    d = jnp.arange(n_tiles * per, dtype=i32)
    e_d = tile_expert[d // per]
    used_d = (d // per) < n_used
    a_d = (d - tile_start_e[e_d] * per) * DISPATCH_TILE
    b_d = jnp.minimum(a_d + DISPATCH_TILE, cnt[e_d])
    filled_d = b_d > a_d
    cum_incl = jnp.concatenate([cum[1:, :E], counts[:, :E]], axis=0).astype(i32)
    col = cum_incl.T[e_d]
    t_lo = jnp.sum(col <= a_d[:, None], axis=1)
    t_hi = jnp.sum(col <= (b_d - 1)[:, None], axis=1)
    c_lo = jnp.where(filled_d, t_lo // DISPATCH_CHUNK, 0)
    c_hi = jnp.where(filled_d, t_hi // DISPATCH_CHUNK, 0)
    n_disp = jnp.where(used_d, c_hi - c_lo + 1, 0).astype(i32)
    owner, k, real, wc, start = _flatten_items(n_disp, E * (T // DISPATCH_CHUNK) + n_tiles * per)
    disp = (owner, (c_lo[owner] + k).astype(i32),
            jnp.where(real, jnp.where(k == 0, 1, 2), 0).astype(i32))

    n_tok_tiles = T // COMBINE_TILE
    cum_b = cum[::COMBINE_TILE, :E].astype(i32)
    cum_n = jnp.concatenate([cum_b[1:], cnt[None, :]], axis=0)
    s_a, s_b = off_e[None, :] + cum_b, off_e[None, :] + cum_n
    k_lo, k_hi = s_a // COMBINE_CHUNK, (s_b - 1) // COMBINE_CHUNK
    n_comb = jnp.where(s_b > s_a, k_hi - k_lo + 1, 0).astype(i32).reshape(-1)
    n_comb_slots = n_tiles * (SLOT_TILE // COMBINE_CHUNK) + E * n_tok_tiles
    owner, k, real, wc, start = _flatten_items(n_comb, n_comb_slots)
    tok_tile = owner // E
    first = wc == start[tok_tile * E]
    comb = (tok_tile.astype(i32), (k_lo.reshape(-1)[owner] + k).astype(i32),
            jnp.where(real, jnp.where(first, 1, 2), 0).astype(i32))

    return dict(group_offset=off_e, disp=disp, comb=comb, tile_expert=tile_expert + expert_base,
                n_used=n_used.reshape(1), n_tiles=n_tiles)


def _dispatch_kernel(tile_ref, chunk_ref, flag_ref, rows_ref, hn_ref, o_ref):
    w = pl.program_id(0)
    flag = flag_ref[w]
    rows, cols = o_ref.shape[0], hn_ref.shape[0]

    def gathered():
        slot = (tile_ref[w] * rows + lax.broadcasted_iota(jnp.int32, (rows, cols), 0)).astype(F32)
        hit = (rows_ref[_POS1:_POS1 + 1, :] == slot) | (rows_ref[_POS2:_POS2 + 1, :] == slot)
        return _dot(jnp.where(hit, 1.0, 0.0).astype(BF16), hn_ref[...]).astype(BF16)

    @pl.when(flag == 1)
    def _():
        o_ref[...] = gathered()

    @pl.when(flag == 2)
    def _():
        o_ref[...] += gathered()


def dispatch_tokens(hn, rows, plan):
    T, D = hn.shape
    tile, chunk, flag = plan["disp"]
    return pl.pallas_call(
        _dispatch_kernel,
        grid_spec=pltpu.PrefetchScalarGridSpec(
            num_scalar_prefetch=3,
            grid=(tile.shape[0],),
            in_specs=[pl.BlockSpec((ROUTE_ROWS, DISPATCH_CHUNK), lambda w, t, c, f: (0, c[w])),
                      pl.BlockSpec((DISPATCH_CHUNK, D), lambda w, t, c, f: (c[w], 0))],
            out_specs=pl.BlockSpec((DISPATCH_TILE, D), lambda w, t, c, f: (t[w], 0)),
        ),
        out_shape=jax.ShapeDtypeStruct((plan["n_tiles"] * SLOT_TILE, D), BF16),
        compiler_params=_params("arbitrary"),
        name="moe_dispatch",
    )(tile, chunk, flag, rows, hn)


def _expert_kernel(te_ref, nu_ref, x_ref, wg_ref, wu_ref, wd_ref, o_ref, acc_ref):
    r, f = pl.program_id(0), pl.program_id(1)

    @pl.when(f == 0)
    def _():
        acc_ref[...] = jnp.zeros_like(acc_ref)

    @pl.when(r < nu_ref[0])
    def _():
        x = x_ref[...]
        a = _dot(x, wg_ref[0])
        b = _dot(x, wu_ref[0])
        acc_ref[...] += _dot((a * jax.nn.sigmoid(a) * b).astype(BF16), wd_ref[0])

    @pl.when(f == pl.num_programs(1) - 1)
    def _():
        o_ref[...] = acc_ref[...].astype(BF16)


def expert_swiglu(xs, plan, w_gate, w_up, w_down, *, tf=512):
    S, D = xs.shape
    FF = w_gate.shape[2]
    tf = min(tf, FF)
    assert FF % tf == 0
    nf = FF // tf
    last = lambda r, nu: jnp.minimum(r, nu[0] - 1)
    col = lambda r, f, nu: jnp.where(r < nu[0], f, nf - 1)
    return pl.pallas_call(
        _expert_kernel,
        grid_spec=pltpu.PrefetchScalarGridSpec(
            num_scalar_prefetch=2,
            grid=(S // SLOT_TILE, nf),
            in_specs=[pl.BlockSpec((SLOT_TILE, D), lambda r, f, te, nu: (last(r, nu), 0)),
                      pl.BlockSpec((1, D, tf), lambda r, f, te, nu: (te[last(r, nu)], 0, col(r, f, nu))),
                      pl.BlockSpec((1, D, tf), lambda r, f, te, nu: (te[last(r, nu)], 0, col(r, f, nu))),
                      pl.BlockSpec((1, tf, D), lambda r, f, te, nu: (te[last(r, nu)], col(r, f, nu), 0))],
            out_specs=pl.BlockSpec((SLOT_TILE, D), lambda r, f, te, nu: (r, 0)),
            scratch_shapes=[pltpu.VMEM((SLOT_TILE, D), F32)],
        ),
        out_shape=jax.ShapeDtypeStruct((S, D), BF16),
        compiler_params=_params("arbitrary", "arbitrary"),
        name="moe_experts",
    )(plan["tile_expert"], plan["n_used"], xs, w_gate, w_up, w_down)


def _combine_kernel(tile_ref, chunk_ref, flag_ref, rec_ref, y_ref, x_ref, o_ref):
    w = pl.program_id(0)
    flag = flag_ref[w]
    rows, cols = o_ref.shape[0], y_ref.shape[0]

    def weighted():
        rec = rec_ref[...]
        slot = (chunk_ref[w] * cols + lax.broadcasted_iota(jnp.int32, (rows, cols), 1)).astype(F32)
        sel = (jnp.where(rec[:, _POS1:_POS1 + 1] == slot, rec[:, _GATE1:_GATE1 + 1], 0.0)
               + jnp.where(rec[:, _POS2:_POS2 + 1] == slot, rec[:, _GATE2:_GATE2 + 1], 0.0))
        return _dot(sel.astype(BF16), y_ref[...])

    @pl.when(flag == 1)
    def _():
        o_ref[...] = x_ref[...] + weighted()

    @pl.when(flag == 2)
    def _():
        o_ref[...] += weighted()


def combine_tokens(x, y, rec, plan):
    T, D = x.shape
    tile, chunk, flag = plan["comb"]
    return pl.pallas_call(
        _combine_kernel,
        grid_spec=pltpu.PrefetchScalarGridSpec(
            num_scalar_prefetch=3,
            grid=(tile.shape[0],),
            in_specs=[pl.BlockSpec((COMBINE_TILE, LANES), lambda w, t, c, f: (t[w], 0)),
                      pl.BlockSpec((COMBINE_CHUNK, D), lambda w, t, c, f: (c[w], 0)),
                      pl.BlockSpec((COMBINE_TILE, D), lambda w, t, c, f: (t[w], 0))],
            out_specs=pl.BlockSpec((COMBINE_TILE, D), lambda w, t, c, f: (t[w], 0)),
        ),
        out_shape=jax.ShapeDtypeStruct((T, D), F32),
        compiler_params=_params("arbitrary"),
        name="moe_combine",
    )(tile, chunk, flag, rec, y, x)


def moe_residual(x, g, router_w, w_gate, w_up, w_down, layer):
    E = router_w.shape[1]
    hn, meta, cum, counts = route_tokens(x, g, router_w)
    plan = _moe_plan(cum, counts, n_experts=E, expert_base=layer * E)
    rec, rows = assign_slots(meta, plan["group_offset"], n_experts=E)
    xs = dispatch_tokens(hn, rows, plan)
    y = expert_swiglu(xs, plan, w_gate, w_up, w_down)
    return combine_tokens(x, y, rec, plan)


def _final_norm_kernel(x_ref, g_ref, o_ref):
    o_ref[...] = _rms(x_ref[...], g_ref[...])


def final_norm(x, g, *, tm=1024):
    T, D = x.shape
    tm = min(tm, T)
    return pl.pallas_call(
        _final_norm_kernel,
        grid=(T // tm,),
        in_specs=[pl.BlockSpec((tm, D), lambda i: (i, 0)), pl.BlockSpec((1, D), lambda i: (0, 0))],
        out_specs=pl.BlockSpec((tm, D), lambda i: (i, 0)),
        out_shape=jax.ShapeDtypeStruct((T, D), F32),
        compiler_params=_params("parallel"),
        name="final_norm",
    )(x, g.reshape(1, D))


def kernel(x, mem, positions, mem_norm_g, final_norm_g, mix_norm_g, xa_norm_g, ffn_norm_g, xa_wq, xa_wkv, xa_wo, ev_w_in, ev_conv_w, ev_ret_decay_f, ev_ret_decay_b, ev_ret_gn_g, ev_w_out, ffd_w_gate, ffd_w_up, ffd_w_down, od_w_in, od_lam_q1, od_lam_k1, od_lam_q2, od_lam_k2, od_gn_g, od_w_out, moe_router, moe_w_gate, moe_w_up, moe_w_down):
    B, L, D = x.shape
    M = mem.shape[1]
    T = B * L
    depth = mix_norm_g.shape[0]
    conv_ch = ev_conv_w.shape[2]
    ret_w = ev_ret_gn_g.shape[1]
    ret_dk = ret_w // RET_HEADS
    diff_d = od_lam_q1.shape[1]
    bf = lambda w: w.astype(BF16)
    experts = lambda w: bf(w).reshape((-1,) + w.shape[2:])

    xa_wq, xa_wkv, xa_wo = bf(xa_wq), bf(xa_wkv), bf(xa_wo)
    ev_w_in, ev_w_out, od_w_in, od_w_out = bf(ev_w_in), bf(ev_w_out), bf(od_w_in), bf(od_w_out)
    ffd_w_gate, ffd_w_up, ffd_w_down = bf(ffd_w_gate), bf(ffd_w_up), bf(ffd_w_down)
    moe_w_gate, moe_w_up, moe_w_down = experts(moe_w_gate), experts(moe_w_up), experts(moe_w_down)
    ev_conv_w = ev_conv_w.astype(F32)

    ret_cos, ret_sin = _rope_tables(positions, ret_dk, RET_THETA)
    diff_cos, diff_sin = _rope_tables(positions, diff_d // 4, ROPE_THETA)
    mem2 = mem.reshape(B * M, D)
    xt = x.reshape(T, D)

    for i in range(depth):
        j = i // 2
        if i % 2 == 0:
            proj = norm_proj(xt, mix_norm_g[i], ev_w_in, j).reshape(B, L, -1)
            a = short_conv(proj, ev_conv_w, j, ch=conv_ch)
            r = retention(proj, ret_cos, ret_sin, ev_ret_decay_f[j], ev_ret_decay_b[j], ev_ret_gn_g[j],
                          col0=3 * conv_ch // ret_w, width=ret_w, dk=ret_dk)
            parts, w_out = [a, r], ev_w_out
        else:
            lambda_init = 0.8 - 0.6 * math.exp(-0.3 * i)
            f32 = lambda v: v.astype(F32)
            lam = (jnp.exp(jnp.sum(f32(od_lam_q1[j]) * f32(od_lam_k1[j])))
                   - jnp.exp(jnp.sum(f32(od_lam_q2[j]) * f32(od_lam_k2[j]))) + lambda_init)
            proj = norm_proj(xt, mix_norm_g[i], od_w_in, j).reshape(B, L, -1)
            o = diff_attention(proj, diff_cos, diff_sin, lam, od_gn_g[j], rot_dim=diff_d // 4,
                               lambda_init=lambda_init, d=diff_d)
            parts, w_out = [o], od_w_out

        kv = norm_proj(mem2, mem_norm_g, xa_wkv, i).reshape(B, M, 2 * D)
        xt = mixer_out_cross_attention(xt.reshape(B, L, D), parts, w_out, j, xa_norm_g[i], xa_wq, kv, xa_wo,
                                       i).reshape(T, D)

        if i % 2 == 0:
            xt = swiglu_residual(xt, ffn_norm_g[i], ffd_w_gate, ffd_w_up, ffd_w_down, j)
        else:
            xt = moe_residual(xt, ffn_norm_g[i], moe_router[j], moe_w_gate, moe_w_up, moe_w_down, j)

    return final_norm(xt, final_norm_g).reshape(B, L, D)
```

```python
import functools
import math

import jax
import jax.numpy as jnp
from jax import lax
from jax.experimental import pallas as pl
from jax.experimental.pallas import tpu as pltpu

F32 = jnp.float32
BF16 = jnp.bfloat16
EPS = 1e-6
LANES = 128
LOG2E = 1.4426950408889634

RET_HEADS = 8
RET_THETA = 10000.0
RET_CHUNK = 256
ROPE_THETA = 500000.0
XA_HEADS = 4

VMEM_LIMIT = 56 * 1024 * 1024


def _params(*sem, vmem_limit=VMEM_LIMIT):
    return pltpu.CompilerParams(dimension_semantics=sem, vmem_limit_bytes=vmem_limit)


def _rms(x, g):
    ms = jnp.mean(x * x, axis=-1, keepdims=True)
    return x * lax.rsqrt(ms + EPS) * g


def _dot(a, b):
    return jnp.dot(a, b, preferred_element_type=F32)


def _dot_nt(a, b):
    return lax.dot_general(a, b, (((1,), (1,)), ((), ())), preferred_element_type=F32)


def _rotate_pairs(x, cos_t, sin_t, half):
    lane = lax.broadcasted_iota(jnp.int32, x.shape, 1) % 64
    partner = jnp.where(lane < half, pltpu.roll(x, LANES - half, axis=1), pltpu.roll(x, half, axis=1))
    return x * cos_t + partner * sin_t


def _layer_spec(w, layer):
    return pl.BlockSpec((1,) + w.shape[1:], lambda *_: (layer, 0, 0))


def _norm_proj_kernel(x_ref, g_ref, w_ref, o_ref, *, tn):
    hn = _rms(x_ref[...], g_ref[...]).astype(BF16)
    for c in range(o_ref.shape[1] // tn):
        cols = slice(c * tn, (c + 1) * tn)
        o_ref[:, cols] = _dot(hn, w_ref[0, :, cols]).astype(o_ref.dtype)


def norm_proj(x, g, w, layer, *, tm=1024, tn=512):
    T, D = x.shape
    N = w.shape[2]
    tm, tn = min(tm, T), min(tn, N)
    assert T % tm == 0 and N % tn == 0
    return pl.pallas_call(
        functools.partial(_norm_proj_kernel, tn=tn),
        grid=(T // tm,),
        in_specs=[pl.BlockSpec((tm, D), lambda i: (i, 0)),
                  pl.BlockSpec((1, D), lambda i: (0, 0)),
                  _layer_spec(w, layer)],
        out_specs=pl.BlockSpec((tm, N), lambda i: (i, 0)),
        out_shape=jax.ShapeDtypeStruct((T, N), BF16),
        compiler_params=_params("parallel"),
        name="norm_proj",
    )(x, g.reshape(1, D), w)


def _conv_kernel(gb_ref, gc_ref, h_ref, w_ref, o_ref):
    u = gc_ref[0].astype(F32) * h_ref[0].astype(F32)
    L = u.shape[0]
    row = lax.broadcasted_iota(jnp.int32, u.shape, 0)
    prev = jnp.where(row == 0, 0.0, pltpu.roll(u, 1, axis=0))
    nxt = jnp.where(row == L - 1, 0.0, pltpu.roll(u, L - 1, axis=0))
    conv = w_ref[0, 0:1, :] * prev + w_ref[0, 1:2, :] * u + w_ref[0, 2:3, :] * nxt
    o_ref[0] = (gb_ref[0].astype(F32) * conv).astype(BF16)


def short_conv(proj, conv_w, layer, *, ch):
    B, L, _ = proj.shape
    col = lambda c: pl.BlockSpec((1, L, ch), lambda b: (b, 0, c))
    return pl.pallas_call(
        _conv_kernel,
        grid=(B,),
        in_specs=[col(0), col(1), col(2), _layer_spec(conv_w, layer)],
        out_specs=pl.BlockSpec((1, L, ch), lambda b: (b, 0, 0)),
        out_shape=jax.ShapeDtypeStruct((B, L, ch), BF16),
        compiler_params=_params("parallel"),
        name="short_conv",
    )(proj, proj, proj, conv_w)


def _retention_kernel(q_ref, k_ref, v_ref, g_ref, cos_ref, sin_ref, dmat_ref, qdec_ref, kdecf_ref,
                      kdecb_ref, cdecf_ref, cdecb_ref, gn_ref, o_ref, qrot_ref, krot_ref, bst_ref,
                      fstate_ref, bstate_ref, *, chunk, dk):
    L = q_ref.shape[1]
    n_pairs = q_ref.shape[2] // LANES
    nc = L // chunk
    C = chunk
    cos_t, sin_t = cos_ref[0], sin_ref[0]

    for p in range(n_pairs):
        cols = slice(p * LANES, (p + 1) * LANES)
        qrot_ref[:, cols] = _rotate_pairs(q_ref[0, :, cols].astype(F32), cos_t, sin_t, dk // 2).astype(BF16)
        kr = _rotate_pairs(k_ref[0, :, cols].astype(F32), cos_t, sin_t, dk // 2)
        krot_ref[:, cols] = (kr * (dk ** -0.5)).astype(BF16)

    lane_c = lax.broadcasted_iota(jnp.int32, (C, LANES), 1)
    first_head = lane_c < dk
    r_i = lax.broadcasted_iota(jnp.int32, (LANES, LANES), 0) // dk
    c_i = lax.broadcasted_iota(jnp.int32, (LANES, LANES), 1) // dk
    same_head = r_i == c_i

    def kv_outer(kp, kdec, vp):
        kd_t = (kp.astype(F32) * kdec).T.astype(BF16)
        return jnp.where(same_head, _dot(kd_t, vp), 0.0)

    bstate_ref[...] = jnp.zeros_like(bstate_ref)

    def back_body(t, carry):
        i = nc - 1 - t
        rows = pl.ds(pl.multiple_of(i * C, C), C)
        for p in range(n_pairs):
            cols = slice(p * LANES, (p + 1) * LANES)
            bst_ref[i, p] = bstate_ref[p]
            upd = kv_outer(krot_ref[rows, cols], kdecb_ref[p], v_ref[0, rows, cols])
            bstate_ref[p] = cdecb_ref[p] * bstate_ref[p] + upd
        return carry

    lax.fori_loop(0, nc, back_body, 0)

    fstate_ref[...] = jnp.zeros_like(fstate_ref)

    def fwd_body(i, carry):
        rows = pl.ds(pl.multiple_of(i * C, C), C)
        for p in range(n_pairs):
            cols = slice(p * LANES, (p + 1) * LANES)
            qp, kp, vp = qrot_ref[rows, cols], krot_ref[rows, cols], v_ref[0, rows, cols]
            zero = jnp.zeros_like(qp)
            s0 = _dot_nt(jnp.where(first_head, qp, zero), kp) * dmat_ref[2 * p]
            s1 = _dot_nt(jnp.where(first_head, zero, qp), kp) * dmat_ref[2 * p + 1]
            intra = jnp.where(first_head, _dot(s0.astype(BF16), vp), _dot(s1.astype(BF16), vp))
            q32 = qp.astype(F32)
            qd = qdec_ref[p]
            qq = jnp.concatenate([q32 * qd[:, :LANES], q32 * qd[:, LANES:]], axis=1).astype(BF16)
            st = jnp.concatenate([fstate_ref[p], bst_ref[i, p]], axis=0).astype(BF16)
            o = intra + _dot(qq, st)
            o2 = o * o
            s_first = jnp.sum(jnp.where(first_head, o2, 0.0), axis=-1, keepdims=True)
            s_all = jnp.sum(o2, axis=-1, keepdims=True)
            ms = jnp.where(first_head, s_first, s_all - s_first) * (1.0 / dk)
            on = o * lax.rsqrt(ms + EPS) * gn_ref[:, cols]
            g = g_ref[0, rows, cols].astype(F32)
            o_ref[0, rows, cols] = (g * jax.nn.sigmoid(g) * on).astype(BF16)
            upd = kv_outer(kp, kdecf_ref[p], vp)
            fstate_ref[p] = cdecf_ref[p] * fstate_ref[p] + upd
        return carry

    lax.fori_loop(0, nc, fwd_body, 0)


def _retention_tables(p_fwd, p_bwd, C, dk):
    H = p_fwd.shape[0]
    lg_f = -jnp.exp(p_fwd.astype(F32))
    lg_b = -jnp.exp(p_bwd.astype(F32))
    idx = jnp.arange(C, dtype=F32)
    dist = idx[:, None] - idx[None, :]
    dmat = jnp.where(dist >= 0,
                     jnp.exp(lg_f[:, None, None] * jnp.maximum(dist, 0.0)),
                     jnp.exp(lg_b[:, None, None] * jnp.maximum(-dist, 0.0)))

    def lanes(per_head):
        t = jnp.repeat(per_head[:, :, None], dk, axis=2)
        return t.reshape(H // 2, 2, C, dk).transpose(0, 2, 1, 3).reshape(H // 2, C, 2 * dk)

    qdec_f = lanes(jnp.exp(lg_f[:, None] * (idx + 1.0)))
    qdec_b = lanes(jnp.exp(lg_b[:, None] * (C - idx)))
    kdec_f = lanes(jnp.exp(lg_f[:, None] * (C - 1.0 - idx)))
    kdec_b = lanes(jnp.exp(lg_b[:, None] * idx))
    qdec = jnp.concatenate([qdec_f, qdec_b], axis=2)

    def rows(lg):
        per_row = jnp.repeat(jnp.exp(lg * C).reshape(H // 2, 2), dk, axis=1)
        return jnp.broadcast_to(per_row[:, :, None], (H // 2, 2 * dk, 2 * dk))

    return dmat, qdec, kdec_f, kdec_b, rows(lg_f), rows(lg_b)


def _rope_tables(positions, rot_dim, theta, head_dim=64):
    half = rot_dim // 2
    j = jnp.arange(LANES) % head_dim
    inv_freq = jnp.exp(-math.log(theta) * (j % half).astype(F32) * (2.0 / rot_dim))
    ang = positions.astype(F32)[:, :, None] * inv_freq
    cos_t = jnp.where(j < rot_dim, jnp.cos(ang), 1.0)
    sin_t = jnp.where(j < rot_dim, jnp.where(j < half, -jnp.sin(ang), jnp.sin(ang)), 0.0)
    return cos_t, sin_t


def retention(proj, cos_t, sin_t, p_fwd, p_bwd, gn_g, *, col0, width, dk=64):
    B, L, _ = proj.shape
    C = min(RET_CHUNK, L)
    assert L % C == 0 and width % LANES == 0
    n_pairs = width // LANES
    nc = L // C
    tables = _retention_tables(p_fwd, p_bwd, C, dk)
    col = lambda c: pl.BlockSpec((1, L, width), lambda b: (b, 0, col0 + c))
    rope = pl.BlockSpec((1, L, LANES), lambda b: (b, 0, 0))
    full = lambda a: pl.BlockSpec(a.shape, lambda b: (0,) * a.ndim)
    gn = gn_g.reshape(1, width).astype(F32)
    return pl.pallas_call(
        functools.partial(_retention_kernel, chunk=C, dk=dk),
        grid=(B,),
        in_specs=[col(0), col(1), col(2), col(3), rope, rope] + [full(t) for t in tables] + [full(gn)],
        out_specs=pl.BlockSpec((1, L, width), lambda b: (b, 0, 0)),
        out_shape=jax.ShapeDtypeStruct((B, L, width), BF16),
        scratch_shapes=[pltpu.VMEM((L, width), BF16), pltpu.VMEM((L, width), BF16),
                        pltpu.VMEM((nc, n_pairs, LANES, LANES), F32),
                        pltpu.VMEM((n_pairs, LANES, LANES), F32),
                        pltpu.VMEM((n_pairs, LANES, LANES), F32)],
        compiler_params=_params("parallel"),
        name="retention",
    )(proj, proj, proj, proj, cos_t, sin_t, *tables, gn)


N_SCORE_BUFS = 2
DIFF_ATTN_VMEM_LIMIT = 60 * 1024 * 1024


def _diff_attn_kernel(lam_ref, q_ref, k_ref, v_ref, cosk_ref, sink_ref, cosq_ref, sinq_ref, gn_ref, perm_ref,
                      o_ref, krot_ref, s_ref, e_ref, *, d, post_scale):
    n_heads = q_ref.shape[2] // LANES
    L = k_ref.shape[1]

    def rotate(x, cos_t, sin_t):
        return x.astype(F32) * cos_t + _dot(x, perm_ref[...]) * sin_t

    @pl.when(pl.program_id(1) == 0)
    def _():
        for h in range(n_heads):
            cols = slice(h * LANES, (h + 1) * LANES)
            krot_ref[:, cols] = rotate(k_ref[0, :, cols], cosk_ref[0], sink_ref[0]).astype(BF16)

    lam = lam_ref[0, 0]
    tq = q_ref.shape[1]
    first_map = lax.broadcasted_iota(jnp.int32, (tq, LANES), 1) < d

    def attend(q, kp, vp, buf):
        s_ref[buf] = _dot_nt(q.astype(BF16), kp)
        block = lambda c: slice(c * LANES, (c + 1) * LANES)
        m_part = s_ref[buf, :, block(0)]
        for c in range(1, L // LANES):
            m_part = jnp.maximum(m_part, s_ref[buf, :, block(c)])
        m = jnp.max(m_part, axis=-1, keepdims=True)
        l_part = jnp.zeros((tq, LANES), F32)
        for c in range(L // LANES):
            e = jnp.exp2(s_ref[buf, :, block(c)] - m)
            l_part = l_part + e
            e_ref[buf, :, block(c)] = e.astype(BF16)
        return _dot(e_ref[buf], vp) / jnp.sum(l_part, axis=-1, keepdims=True)

    for h in range(n_heads):
        cols = slice(h * LANES, (h + 1) * LANES)
        qr = rotate(q_ref[0, :, cols], cosq_ref[0], sinq_ref[0]) * (d ** -0.5 * LOG2E)
        kp, vp = krot_ref[:, cols], v_ref[0, :, cols]
        o1 = attend(jnp.where(first_map, qr, 0.0), kp, vp, (2 * h) % N_SCORE_BUFS)
        o2 = attend(jnp.where(first_map, 0.0, qr), kp, vp, (2 * h + 1) % N_SCORE_BUFS)
        on = _rms(o1 - lam * o2, gn_ref[:, cols]) * post_scale
        o_ref[0, :, cols] = on.astype(BF16)


def _partner_permutation(rot_dim, head_dim=64):
    half = rot_dim // 2
    j = jnp.arange(LANES)
    jj = j % head_dim
    partner = jnp.where(jj < half, j + half, j - half)
    p = (jnp.arange(LANES)[:, None] == partner[None, :]) & (jj < rot_dim)[None, :]
    return p.astype(BF16)


def diff_attention(proj, cos_t, sin_t, lam, gn_g, *, rot_dim, lambda_init, d=64, tq=256):
    B, L, N = proj.shape
    W = N // 3
    tq = min(tq, L)
    assert L % tq == 0
    col = lambda c, rows, f: pl.BlockSpec((1, rows, W), f(c))
    q_map = lambda c: (lambda b, i: (b, i, c))
    kv_map = lambda c: (lambda b, i: (b, 0, c))
    return pl.pallas_call(
        functools.partial(_diff_attn_kernel, d=d, post_scale=1.0 - lambda_init),
        grid=(B, L // tq),
        in_specs=[pl.BlockSpec(memory_space=pltpu.SMEM),
                  col(0, tq, q_map), col(1, L, kv_map), col(2, L, kv_map),
                  pl.BlockSpec((1, L, LANES), lambda b, i: (b, 0, 0)),
                  pl.BlockSpec((1, L, LANES), lambda b, i: (b, 0, 0)),
                  pl.BlockSpec((1, tq, LANES), lambda b, i: (b, i, 0)),
                  pl.BlockSpec((1, tq, LANES), lambda b, i: (b, i, 0)),
                  pl.BlockSpec((1, W), lambda b, i: (0, 0)),
                  pl.BlockSpec((LANES, LANES), lambda b, i: (0, 0))],
        out_specs=pl.BlockSpec((1, tq, W), lambda b, i: (b, i, 0)),
        out_shape=jax.ShapeDtypeStruct((B, L, W), BF16),
        scratch_shapes=[pltpu.VMEM((L, W), BF16), pltpu.VMEM((N_SCORE_BUFS, tq, L), F32),
                        pltpu.VMEM((N_SCORE_BUFS, tq, L), BF16)],
        compiler_params=_params("parallel", "arbitrary", vmem_limit=DIFF_ATTN_VMEM_LIMIT),
        name="diff_attention",
    )(lam.reshape(1, 1).astype(F32), proj, proj, proj, cos_t, sin_t, cos_t, sin_t,
      gn_g.reshape(1, W).astype(F32), _partner_permutation(rot_dim, d))


def _mix_xattn_kernel(*refs, n_parts, heads):
    parts, w_out_ref = refs[:n_parts], refs[n_parts]
    x_ref, g_ref, wq_ref, kv_ref, wo_ref, o_ref = refs[n_parts + 1:]
    x = x_ref[0]
    D = x.shape[1]
    hd = D // heads
    row = 0
    for a_ref in parts:
        k = a_ref.shape[2]
        x = x + _dot(a_ref[0], w_out_ref[0, row:row + k, :])
        row += k
    hn = _rms(x, g_ref[...]).astype(BF16)
    q = (_dot(hn, wq_ref[0]) * (hd ** -0.5)).astype(BF16)
    outs = []
    for h in range(heads):
        kh = kv_ref[0, :, h * hd:(h + 1) * hd]
        vh = kv_ref[0, :, D + h * hd:D + (h + 1) * hd]
        s = _dot_nt(q[:, h * hd:(h + 1) * hd], kh)
        e = jnp.exp(s - jnp.max(s, axis=-1, keepdims=True))
        oh = _dot(e.astype(BF16), vh) / jnp.sum(e, axis=-1, keepdims=True)
        outs.append(oh.astype(BF16))
    o_ref[0] = x + _dot(jnp.concatenate(outs, axis=-1), wo_ref[0])


def mixer_out_cross_attention(x, parts, w_out, mix_layer, g, wq, kv, wo, layer, *, tq=512):
    B, L, D = x.shape
    M = kv.shape[1]
    tq = min(tq, L)
    assert L % tq == 0
    tile = lambda width: pl.BlockSpec((1, tq, width), lambda b, i: (b, i, 0))
    return pl.pallas_call(
        functools.partial(_mix_xattn_kernel, n_parts=len(parts), heads=XA_HEADS),
        grid=(B, L // tq),
        in_specs=[tile(a.shape[2]) for a in parts] + [
            _layer_spec(w_out, mix_layer), tile(D), pl.BlockSpec((1, D), lambda b, i: (0, 0)),
            _layer_spec(wq, layer), pl.BlockSpec((1, M, 2 * D), lambda b, i: (b, 0, 0)),
            _layer_spec(wo, layer)],
        out_specs=tile(D),
        out_shape=jax.ShapeDtypeStruct((B, L, D), F32),
        compiler_params=_params("parallel", "parallel"),
        name="mixer_out_cross_attention",
    )(*parts, w_out, x, g.reshape(1, D), wq, kv, wo)


def _swiglu_kernel(x_ref, g_ref, wg_ref, wu_ref, wd_ref, o_ref, *, tf):
    x = x_ref[...]
    hn = _rms(x, g_ref[...]).astype(BF16)
    acc = x
    for c in range(wg_ref.shape[2] // tf):
        cols = slice(c * tf, (c + 1) * tf)
        a = _dot(hn, wg_ref[0, :, cols])
        b = _dot(hn, wu_ref[0, :, cols])
        acc = acc + _dot((a * jax.nn.sigmoid(a) * b).astype(BF16), wd_ref[0, cols, :])
    o_ref[...] = acc


def swiglu_residual(x, g, w_gate, w_up, w_down, layer, *, tm=512, tf=256):
    T, D = x.shape
    FF = w_gate.shape[2]
    tm, tf = min(tm, T), min(tf, FF)
    assert T % tm == 0 and FF % tf == 0
    return pl.pallas_call(
        functools.partial(_swiglu_kernel, tf=tf),
        grid=(T // tm,),
        in_specs=[pl.BlockSpec((tm, D), lambda i: (i, 0)),
                  pl.BlockSpec((1, D), lambda i: (0, 0)),
                  _layer_spec(w_gate, layer), _layer_spec(w_up, layer), _layer_spec(w_down, layer)],
        out_specs=pl.BlockSpec((tm, D), lambda i: (i, 0)),
        out_shape=jax.ShapeDtypeStruct((T, D), F32),
        compiler_params=_params("parallel"),
        name="swiglu_residual",
    )(x, g.reshape(1, D), w_gate, w_up, w_down)


SLOT_TILE = 1024
DISPATCH_TILE = 256
DISPATCH_CHUNK = 1024
COMBINE_TILE = 512
COMBINE_PIECE = 128
ROUTE_ROWS = 8
_E1, _E2, _W1, _W2, _RANK1, _RANK2 = range(6)
_POS1, _POS2, _GATE1, _GATE2 = range(4)


def _router_kernel(x_ref, g_ref, wr_ref, hn_ref, meta_ref, cum_ref, counts_ref, run_ref, *, n_experts):
    @pl.when(pl.program_id(0) == 0)
    def _():
        run_ref[...] = jnp.zeros_like(run_ref)

    hn = _rms(x_ref[...], g_ref[...])
    hn_ref[...] = hn.astype(BF16)
    logits = jnp.dot(hn, wr_ref[...], preferred_element_type=F32, precision=lax.Precision.HIGHEST)
    tm = logits.shape[0]
    lane = lax.broadcasted_iota(jnp.int32, logits.shape, 1)
    neg = jnp.float32(-jnp.inf)
    logits = jnp.where(lane < n_experts, logits, neg)
    m1 = jnp.max(logits, axis=-1, keepdims=True)
    i1 = jnp.min(jnp.where(logits == m1, lane, LANES), axis=-1, keepdims=True)
    rest = jnp.where(lane == i1, neg, logits)
    m2 = jnp.max(rest, axis=-1, keepdims=True)
    i2 = jnp.min(jnp.where(rest == m2, lane, LANES), axis=-1, keepdims=True)
    e2 = jnp.exp(m2 - m1)
    w1 = 1.0 / (1.0 + e2)
    w2 = e2 / (1.0 + e2)

    chosen = jnp.where((lane == i1) | (lane == i2), 1.0, 0.0)
    r_i = lax.broadcasted_iota(jnp.int32, (tm, tm), 0)
    c_i = lax.broadcasted_iota(jnp.int32, (tm, tm), 1)
    tri = jnp.where(c_i < r_i, 1.0, 0.0).astype(BF16)
    cum = _dot(tri, chosen.astype(BF16)) + run_ref[...]
    cum_ref[...] = cum
    rank1 = jnp.sum(jnp.where(lane == i1, cum, 0.0), axis=-1, keepdims=True)
    rank2 = jnp.sum(jnp.where(lane == i2, cum, 0.0), axis=-1, keepdims=True)
    run_ref[...] += jnp.sum(chosen, axis=0, keepdims=True)
    counts_ref[...] = run_ref[...]

    meta = jnp.zeros_like(logits)
    for idx, val in ((_E1, i1.astype(F32)), (_E2, i2.astype(F32)), (_W1, w1), (_W2, w2),
                     (_RANK1, rank1), (_RANK2, rank2)):
        meta = jnp.where(lane == idx, val, meta)
    meta_ref[...] = meta


def route_tokens(x, g, router_w, *, tm=512):
    T, D = x.shape
    E = router_w.shape[1]
    tm = min(tm, T)
    wr = jnp.pad(router_w.astype(F32), ((0, 0), (0, LANES - E)))
    row = pl.BlockSpec((tm, LANES), lambda i: (i, 0))
    return pl.pallas_call(
        functools.partial(_router_kernel, n_experts=E),
        grid=(T // tm,),
        in_specs=[pl.BlockSpec((tm, D), lambda i: (i, 0)),
                  pl.BlockSpec((1, D), lambda i: (0, 0)),
                  pl.BlockSpec((D, LANES), lambda i: (0, 0))],
        out_specs=[pl.BlockSpec((tm, D), lambda i: (i, 0)), row, row,
                   pl.BlockSpec((1, LANES), lambda i: (0, 0))],
        out_shape=[jax.ShapeDtypeStruct((T, D), BF16), jax.ShapeDtypeStruct((T, LANES), F32),
                   jax.ShapeDtypeStruct((T, LANES), F32), jax.ShapeDtypeStruct((1, LANES), F32)],
        scratch_shapes=[pltpu.VMEM((1, LANES), F32)],
        compiler_params=_params("arbitrary"),
        name="route_tokens",
    )(x, g.reshape(1, D), wr)


def _slots_kernel(off_ref, meta_ref, rec_ref, rows_ref, *, n_experts):
    meta = meta_ref[...]
    e1, e2 = meta[:, _E1:_E1 + 1], meta[:, _E2:_E2 + 1]
    off1, off2 = jnp.zeros_like(e1), jnp.zeros_like(e2)
    for e in range(n_experts):
        off = off_ref[e].astype(F32)
        off1 = jnp.where(e1 == e, off, off1)
        off2 = jnp.where(e2 == e, off, off2)
    lane = lax.broadcasted_iota(jnp.int32, meta.shape, 1)
    rec = jnp.zeros_like(meta)
    for idx, val in ((_POS1, off1 + meta[:, _RANK1:_RANK1 + 1]), (_POS2, off2 + meta[:, _RANK2:_RANK2 + 1]),
                     (_GATE1, meta[:, _W1:_W1 + 1]), (_GATE2, meta[:, _W2:_W2 + 1])):
        rec = jnp.where(lane == idx, val, rec)
    rec_ref[...] = rec
    rows_ref[...] = rec.T[:ROUTE_ROWS, :]


def assign_slots(meta, group_offset, *, n_experts, tm=512):
    T = meta.shape[0]
    tm = min(tm, T)
    return pl.pallas_call(
        functools.partial(_slots_kernel, n_experts=n_experts),
        grid_spec=pltpu.PrefetchScalarGridSpec(
            num_scalar_prefetch=1,
            grid=(T // tm,),
            in_specs=[pl.BlockSpec((tm, LANES), lambda i, off: (i, 0))],
            out_specs=[pl.BlockSpec((tm, LANES), lambda i, off: (i, 0)),
                       pl.BlockSpec((ROUTE_ROWS, tm), lambda i, off: (0, i))],
        ),
        out_shape=[jax.ShapeDtypeStruct((T, LANES), F32), jax.ShapeDtypeStruct((ROUTE_ROWS, T), F32)],
        compiler_params=_params("parallel"),
        name="assign_slots",
    )(group_offset, meta)


def _flatten_items(n_items, n_slots):
    i32 = jnp.int32
    end = jnp.cumsum(n_items).astype(i32)
    start = end - n_items
    total = end[-1]
    w = jnp.arange(n_slots, dtype=i32)
    wc = jnp.minimum(w, total - 1)
    owner = jnp.minimum(jnp.sum(wc[:, None] >= end[None, :], axis=1), n_items.shape[0] - 1).astype(i32)
    return owner, wc - start[owner], w < total, wc, start


def _moe_plan(cum, counts, *, n_experts, expert_base):
    i32 = jnp.int32
    T, E = cum.shape[0], n_experts
    cnt = counts[0, :E].astype(i32)
    tiles_e = (cnt + SLOT_TILE - 1) // SLOT_TILE
    tile_end_e = jnp.cumsum(tiles_e).astype(i32)
    tile_start_e = tile_end_e - tiles_e
    n_used = tile_end_e[-1]
    off_e = tile_start_e * SLOT_TILE

    n_tiles = 2 * T // SLOT_TILE + E
    r = jnp.arange(n_tiles, dtype=i32)
    tile_expert = jnp.minimum(jnp.sum(r[:, None] >= tile_end_e[None, :], axis=1), E - 1).astype(i32)

    per = SLOT_TILE // DISPATCH_TILE
    d = jnp.arange(n_tiles * per, dtype=i32)
    e_d = tile_expert[d // per]
    used_d = (d // per) < n_used
    a_d = (d - tile_start_e[e_d] * per) * DISPATCH_TILE
    b_d = jnp.minimum(a_d + DISPATCH_TILE, cnt[e_d])
    filled_d = b_d > a_d
    cum_incl = jnp.concatenate([cum[1:, :E], counts[:, :E]], axis=0).astype(i32)
    col = cum_incl.T[e_d]
    t_lo = jnp.sum(col <= a_d[:, None], axis=1)
    t_hi = jnp.sum(col <= (b_d - 1)[:, None], axis=1)
    c_lo = jnp.where(filled_d, t_lo // DISPATCH_CHUNK, 0)
    c_hi = jnp.where(filled_d, t_hi // DISPATCH_CHUNK, 0)
    n_disp = jnp.where(used_d, c_hi - c_lo + 1, 0).astype(i32)
    owner, k, real, wc, start = _flatten_items(n_disp, E * (T // DISPATCH_CHUNK) + n_tiles * per)
    disp = (owner, (c_lo[owner] + k).astype(i32),
            jnp.where(real, jnp.where(k == 0, 1, 2), 0).astype(i32))

    n_tok_tiles = T // COMBINE_TILE
    cum_b = cum[::COMBINE_TILE, :E].astype(i32)
    cum_n = jnp.concatenate([cum_b[1:], cnt[None, :]], axis=0)
    s_a, s_b = off_e[None, :] + cum_b, off_e[None, :] + cum_n
    p_lo, p_hi = s_a // COMBINE_PIECE, (s_b - 1) // COMBINE_PIECE
    n_piece = jnp.where(s_b > s_a, p_hi - p_lo + 1, 0).astype(i32)
    piece_end = jnp.cumsum(n_piece, axis=1).astype(i32)
    piece_start = piece_end - n_piece
    n_valid = piece_end[:, -1]
    kk = jnp.minimum(jnp.arange(_combine_pieces(E), dtype=i32)[None, :], n_valid[:, None] - 1)
    owner = jnp.sum(kk[:, :, None] >= piece_end[:, None, :], axis=2).astype(i32)
    take = lambda a: jnp.take_along_axis(a, owner, axis=1)
    piece = take(p_lo) + kk - take(piece_start)
    comb = (piece.reshape(-1).astype(i32), n_valid)

    return dict(group_offset=off_e, disp=disp, comb=comb, tile_expert=tile_expert + expert_base,
                n_used=n_used.reshape(1), n_tiles=n_tiles)


def _dispatch_kernel(tile_ref, chunk_ref, flag_ref, rows_ref, hn_ref, o_ref):
    w = pl.program_id(0)
    flag = flag_ref[w]
    rows, cols = o_ref.shape[0], hn_ref.shape[0]

    def gathered():
        slot = (tile_ref[w] * rows + lax.broadcasted_iota(jnp.int32, (rows, cols), 0)).astype(F32)
        hit = (rows_ref[_POS1:_POS1 + 1, :] == slot) | (rows_ref[_POS2:_POS2 + 1, :] == slot)
        return _dot(jnp.where(hit, 1.0, 0.0).astype(BF16), hn_ref[...]).astype(BF16)

    @pl.when(flag == 1)
    def _():
        o_ref[...] = gathered()

    @pl.when(flag == 2)
    def _():
        o_ref[...] += gathered()


def dispatch_tokens(hn, rows, plan):
    T, D = hn.shape
    tile, chunk, flag = plan["disp"]
    return pl.pallas_call(
        _dispatch_kernel,
        grid_spec=pltpu.PrefetchScalarGridSpec(
            num_scalar_prefetch=3,
            grid=(tile.shape[0],),
            in_specs=[pl.BlockSpec((ROUTE_ROWS, DISPATCH_CHUNK), lambda w, t, c, f: (0, c[w])),
                      pl.BlockSpec((DISPATCH_CHUNK, D), lambda w, t, c, f: (c[w], 0))],
            out_specs=pl.BlockSpec((DISPATCH_TILE, D), lambda w, t, c, f: (t[w], 0)),
        ),
        out_shape=jax.ShapeDtypeStruct((plan["n_tiles"] * SLOT_TILE, D), BF16),
        compiler_params=_params("arbitrary"),
        name="moe_dispatch",
    )(tile, chunk, flag, rows, hn)


def _expert_kernel(te_ref, nu_ref, x_ref, wg_ref, wu_ref, wd_ref, o_ref, acc_ref, *, sub):
    r, f = pl.program_id(0), pl.program_id(1)

    @pl.when(f == 0)
    def _():
        acc_ref[...] = jnp.zeros_like(acc_ref)

    @pl.when(r < nu_ref[0])
    def _():
        x = x_ref[...]
        acc = acc_ref[...]
        for c in range(wg_ref.shape[2] // sub):
            cols = slice(c * sub, (c + 1) * sub)
            a = _dot(x, wg_ref[0, :, cols])
            b = _dot(x, wu_ref[0, :, cols])
            acc = acc + _dot((a * jax.nn.sigmoid(a) * b).astype(BF16), wd_ref[0, cols, :])
        acc_ref[...] = acc

    @pl.when(f == pl.num_programs(1) - 1)
    def _():
        o_ref[...] = acc_ref[...].astype(BF16)


def expert_swiglu(xs, plan, w_gate, w_up, w_down, *, tf=1792, sub=256):
    S, D = xs.shape
    FF = w_gate.shape[2]
    tf = tf if FF % tf == 0 else FF
    sub = min(sub, tf)
    assert FF % tf == 0 and tf % sub == 0
    nf = FF // tf
    last = lambda r, nu: jnp.minimum(r, nu[0] - 1)
    col = lambda r, f, nu: jnp.where(r < nu[0], f, nf - 1)
    return pl.pallas_call(
        functools.partial(_expert_kernel, sub=sub),
        grid_spec=pltpu.PrefetchScalarGridSpec(
            num_scalar_prefetch=2,
            grid=(S // SLOT_TILE, nf),
            in_specs=[pl.BlockSpec((SLOT_TILE, D), lambda r, f, te, nu: (last(r, nu), 0)),
                      pl.BlockSpec((1, D, tf), lambda r, f, te, nu: (te[last(r, nu)], 0, col(r, f, nu))),
                      pl.BlockSpec((1, D, tf), lambda r, f, te, nu: (te[last(r, nu)], 0, col(r, f, nu))),
                      pl.BlockSpec((1, tf, D), lambda r, f, te, nu: (te[last(r, nu)], col(r, f, nu), 0))],
            out_specs=pl.BlockSpec((SLOT_TILE, D), lambda r, f, te, nu: (r, 0)),
            scratch_shapes=[pltpu.VMEM((SLOT_TILE, D), F32)],
        ),
        out_shape=jax.ShapeDtypeStruct((S, D), BF16),
        compiler_params=_params("arbitrary", "arbitrary"),
        name="moe_experts",
    )(plan["tile_expert"], plan["n_used"], xs, w_gate, w_up, w_down)


def _combine_pieces(n_experts):
    return 2 * COMBINE_TILE // COMBINE_PIECE + 2 * n_experts


def _combine_kernel(piece_ref, nv_ref, rec_ref, x_ref, g_ref, *rest, n_pieces, normalize):
    y_refs, o_ref = rest[:n_pieces], rest[n_pieces]
    i = pl.program_id(0)
    rows = o_ref.shape[0]
    rec = rec_ref[...]
    pos1, pos2 = rec[:, _POS1:_POS1 + 1], rec[:, _POS2:_POS2 + 1]
    gate1, gate2 = rec[:, _GATE1:_GATE1 + 1], rec[:, _GATE2:_GATE2 + 1]
    lane = lax.broadcasted_iota(jnp.int32, (rows, COMBINE_PIECE), 1)

    def weights(k):
        base = jnp.where(k < nv_ref[i], piece_ref[i * n_pieces + k] * COMBINE_PIECE, -COMBINE_PIECE)
        slot = (base + lane).astype(F32)
        return (jnp.where(pos1 == slot, gate1, 0.0) + jnp.where(pos2 == slot, gate2, 0.0)).astype(BF16)

    o_ref[...] = x_ref[...]
    for k in range(0, n_pieces, 2):
        @pl.when(k < nv_ref[i])
        def _():
            sel = jnp.concatenate([weights(k), weights(k + 1)], axis=1)
            y = jnp.concatenate([y_refs[k][...], y_refs[k + 1][...]], axis=0)
            o_ref[...] += _dot(sel, y)

    if normalize:
        o_ref[...] = _rms(o_ref[...], g_ref[...])


def combine_tokens(x, y, rec, plan, *, n_experts, final_g=None):
    T, D = x.shape
    piece, n_valid = plan["comb"]
    n_pieces = _combine_pieces(n_experts)
    g = jnp.ones((1, D), F32) if final_g is None else final_g.reshape(1, D).astype(F32)
    y_spec = lambda k: pl.BlockSpec((COMBINE_PIECE, D), lambda i, p, nv: (p[i * n_pieces + k], 0))
    return pl.pallas_call(
        functools.partial(_combine_kernel, n_pieces=n_pieces, normalize=final_g is not None),
        grid_spec=pltpu.PrefetchScalarGridSpec(
            num_scalar_prefetch=2,
            grid=(T // COMBINE_TILE,),
            in_specs=[pl.BlockSpec((COMBINE_TILE, LANES), lambda i, p, nv: (i, 0)),
                      pl.BlockSpec((COMBINE_TILE, D), lambda i, p, nv: (i, 0)),
                      pl.BlockSpec((1, D), lambda i, p, nv: (0, 0))] + [y_spec(k) for k in range(n_pieces)],
            out_specs=pl.BlockSpec((COMBINE_TILE, D), lambda i, p, nv: (i, 0)),
        ),
        out_shape=jax.ShapeDtypeStruct((T, D), F32),
        compiler_params=_params("parallel"),
        name="moe_combine",
    )(piece, n_valid, rec, x, g, *([y] * n_pieces))


def moe_residual(x, g, router_w, w_gate, w_up, w_down, layer, *, final_g=None):
    E = router_w.shape[1]
    hn, meta, cum, counts = route_tokens(x, g, router_w)
    plan = _moe_plan(cum, counts, n_experts=E, expert_base=layer * E)
    rec, rows = assign_slots(meta, plan["group_offset"], n_experts=E)
    xs = dispatch_tokens(hn, rows, plan)
    y = expert_swiglu(xs, plan, w_gate, w_up, w_down)
    return combine_tokens(x, y, rec, plan, n_experts=E, final_g=final_g)


def _final_norm_kernel(x_ref, g_ref, o_ref):
    o_ref[...] = _rms(x_ref[...], g_ref[...])


def final_norm(x, g, *, tm=1024):
    T, D = x.shape
    tm = min(tm, T)
    return pl.pallas_call(
        _final_norm_kernel,
        grid=(T // tm,),
        in_specs=[pl.BlockSpec((tm, D), lambda i: (i, 0)), pl.BlockSpec((1, D), lambda i: (0, 0))],
        out_specs=pl.BlockSpec((tm, D), lambda i: (i, 0)),
        out_shape=jax.ShapeDtypeStruct((T, D), F32),
        compiler_params=_params("parallel"),
        name="final_norm",
    )(x, g.reshape(1, D))


def kernel(x, mem, positions, mem_norm_g, final_norm_g, mix_norm_g, xa_norm_g, ffn_norm_g, xa_wq, xa_wkv, xa_wo, ev_w_in, ev_conv_w, ev_ret_decay_f, ev_ret_decay_b, ev_ret_gn_g, ev_w_out, ffd_w_gate, ffd_w_up, ffd_w_down, od_w_in, od_lam_q1, od_lam_k1, od_lam_q2, od_lam_k2, od_gn_g, od_w_out, moe_router, moe_w_gate, moe_w_up, moe_w_down):
    B, L, D = x.shape
    M = mem.shape[1]
    T = B * L
    depth = mix_norm_g.shape[0]
    conv_ch = ev_conv_w.shape[2]
    ret_w = ev_ret_gn_g.shape[1]
    ret_dk = ret_w // RET_HEADS
    diff_d = od_lam_q1.shape[1]
    bf = lambda w: w.astype(BF16)
    experts = lambda w: bf(w).reshape((-1,) + w.shape[2:])

    xa_wq, xa_wkv, xa_wo = bf(xa_wq), bf(xa_wkv), bf(xa_wo)
    ev_w_in, ev_w_out, od_w_in, od_w_out = bf(ev_w_in), bf(ev_w_out), bf(od_w_in), bf(od_w_out)
    ffd_w_gate, ffd_w_up, ffd_w_down = bf(ffd_w_gate), bf(ffd_w_up), bf(ffd_w_down)
    moe_w_gate, moe_w_up, moe_w_down = experts(moe_w_gate), experts(moe_w_up), experts(moe_w_down)
    ev_conv_w = ev_conv_w.astype(F32)

    ret_cos, ret_sin = _rope_tables(positions, ret_dk, RET_THETA)
    diff_cos, diff_sin = _rope_tables(positions, diff_d // 4, ROPE_THETA)
    mem2 = mem.reshape(B * M, D)
    xt = x.reshape(T, D)

    for i in range(depth):
        j = i // 2
        if i % 2 == 0:
            proj = norm_proj(xt, mix_norm_g[i], ev_w_in, j).reshape(B, L, -1)
            a = short_conv(proj, ev_conv_w, j, ch=conv_ch)
            r = retention(proj, ret_cos, ret_sin, ev_ret_decay_f[j], ev_ret_decay_b[j], ev_ret_gn_g[j],
                          col0=3 * conv_ch // ret_w, width=ret_w, dk=ret_dk)
            parts, w_out = [a, r], ev_w_out
        else:
            lambda_init = 0.8 - 0.6 * math.exp(-0.3 * i)
            f32 = lambda v: v.astype(F32)
            lam = (jnp.exp(jnp.sum(f32(od_lam_q1[j]) * f32(od_lam_k1[j])))
                   - jnp.exp(jnp.sum(f32(od_lam_q2[j]) * f32(od_lam_k2[j]))) + lambda_init)
            proj = norm_proj(xt, mix_norm_g[i], od_w_in, j).reshape(B, L, -1)
            o = diff_attention(proj, diff_cos, diff_sin, lam, od_gn_g[j], rot_dim=diff_d // 4,
                               lambda_init=lambda_init, d=diff_d)
            parts, w_out = [o], od_w_out

        kv = norm_proj(mem2, mem_norm_g, xa_wkv, i).reshape(B, M, 2 * D)
        xt = mixer_out_cross_attention(xt.reshape(B, L, D), parts, w_out, j, xa_norm_g[i], xa_wq, kv, xa_wo,
                                       i).reshape(T, D)

        if i % 2 == 0:
            xt = swiglu_residual(xt, ffn_norm_g[i], ffd_w_gate, ffd_w_up, ffd_w_down, j)
        else:
            xt = moe_residual(xt, ffn_norm_g[i], moe_router[j], moe_w_gate, moe_w_up, moe_w_down, j,
                              final_g=final_norm_g if i == depth - 1 else None)

    if depth % 2 == 1:
        xt = final_norm(xt, final_norm_g)
    return xt.reshape(B, L, D)
```

```python
import functools
import math

import jax
import jax.numpy as jnp
from jax import lax
from jax.experimental import pallas as pl
from jax.experimental.pallas import tpu as pltpu

F32 = jnp.float32
BF16 = jnp.bfloat16
EPS = 1e-6
LANES = 128
LOG2E = 1.4426950408889634

RET_HEADS = 8
RET_THETA = 10000.0
RET_CHUNK = 256
ROPE_THETA = 500000.0
XA_HEADS = 4

VMEM_LIMIT = 56 * 1024 * 1024


def _params(*sem, vmem_limit=VMEM_LIMIT):
    return pltpu.CompilerParams(dimension_semantics=sem, vmem_limit_bytes=vmem_limit)


def _rms(x, g):
    ms = jnp.mean(x * x, axis=-1, keepdims=True)
    return x * lax.rsqrt(ms + EPS) * g


def _dot(a, b):
    return jnp.dot(a, b, preferred_element_type=F32)


def _dot_nt(a, b):
    return lax.dot_general(a, b, (((1,), (1,)), ((), ())), preferred_element_type=F32)


def _rotate_pairs(x, cos_t, sin_t, half):
    lane = lax.broadcasted_iota(jnp.int32, x.shape, 1) % 64
    partner = jnp.where(lane < half, pltpu.roll(x, LANES - half, axis=1), pltpu.roll(x, half, axis=1))
    return x * cos_t + partner * sin_t


def _layer_spec(w, layer):
    return pl.BlockSpec((1,) + w.shape[1:], lambda *_: (layer, 0, 0))


def _norm_proj_kernel(x_ref, g_ref, w_ref, o_ref, *, tn):
    hn = _rms(x_ref[...], g_ref[...]).astype(BF16)
    for c in range(o_ref.shape[1] // tn):
        cols = slice(c * tn, (c + 1) * tn)
        o_ref[:, cols] = _dot(hn, w_ref[0, :, cols]).astype(o_ref.dtype)


def norm_proj(x, g, w, layer, *, tm=1024, tn=512):
    T, D = x.shape
    N = w.shape[2]
    tm, tn = min(tm, T), min(tn, N)
    assert T % tm == 0 and N % tn == 0
    return pl.pallas_call(
        functools.partial(_norm_proj_kernel, tn=tn),
        grid=(T // tm,),
        in_specs=[pl.BlockSpec((tm, D), lambda i: (i, 0)),
                  pl.BlockSpec((1, D), lambda i: (0, 0)),
                  _layer_spec(w, layer)],
        out_specs=pl.BlockSpec((tm, N), lambda i: (i, 0)),
        out_shape=jax.ShapeDtypeStruct((T, N), BF16),
        compiler_params=_params("parallel"),
        name="norm_proj",
    )(x, g.reshape(1, D), w)


def _conv_kernel(gb_ref, gc_ref, h_ref, w_ref, o_ref):
    u = gc_ref[0].astype(F32) * h_ref[0].astype(F32)
    L = u.shape[0]
    row = lax.broadcasted_iota(jnp.int32, u.shape, 0)
    prev = jnp.where(row == 0, 0.0, pltpu.roll(u, 1, axis=0))
    nxt = jnp.where(row == L - 1, 0.0, pltpu.roll(u, L - 1, axis=0))
    conv = w_ref[0, 0:1, :] * prev + w_ref[0, 1:2, :] * u + w_ref[0, 2:3, :] * nxt
    o_ref[0] = (gb_ref[0].astype(F32) * conv).astype(BF16)


def short_conv(proj, conv_w, layer, *, ch):
    B, L, _ = proj.shape
    col = lambda c: pl.BlockSpec((1, L, ch), lambda b: (b, 0, c))
    return pl.pallas_call(
        _conv_kernel,
        grid=(B,),
        in_specs=[col(0), col(1), col(2), _layer_spec(conv_w, layer)],
        out_specs=pl.BlockSpec((1, L, ch), lambda b: (b, 0, 0)),
        out_shape=jax.ShapeDtypeStruct((B, L, ch), BF16),
        compiler_params=_params("parallel"),
        name="short_conv",
    )(proj, proj, proj, conv_w)


def _retention_kernel(q_ref, k_ref, v_ref, g_ref, cos_ref, sin_ref, dmat_ref, qdec_ref, kdecf_ref,
                      kdecb_ref, cdecf_ref, cdecb_ref, gn_ref, o_ref, qrot_ref, krot_ref, bst_ref,
                      fstate_ref, bstate_ref, *, chunk, dk):
    L = q_ref.shape[1]
    n_pairs = q_ref.shape[2] // LANES
    nc = L // chunk
    C = chunk
    cos_t, sin_t = cos_ref[0], sin_ref[0]

    for p in range(n_pairs):
        cols = slice(p * LANES, (p + 1) * LANES)
        qrot_ref[:, cols] = _rotate_pairs(q_ref[0, :, cols].astype(F32), cos_t, sin_t, dk // 2).astype(BF16)
        kr = _rotate_pairs(k_ref[0, :, cols].astype(F32), cos_t, sin_t, dk // 2)
        krot_ref[:, cols] = (kr * (dk ** -0.5)).astype(BF16)

    lane_c = lax.broadcasted_iota(jnp.int32, (C, LANES), 1)
    first_head = lane_c < dk
    r_i = lax.broadcasted_iota(jnp.int32, (LANES, LANES), 0) // dk
    c_i = lax.broadcasted_iota(jnp.int32, (LANES, LANES), 1) // dk
    same_head = r_i == c_i

    def kv_outer(kp, kdec, vp):
        kd_t = (kp.astype(F32) * kdec).T.astype(BF16)
        return jnp.where(same_head, _dot(kd_t, vp), 0.0)

    bstate_ref[...] = jnp.zeros_like(bstate_ref)

    def back_body(t, carry):
        i = nc - 1 - t
        rows = pl.ds(pl.multiple_of(i * C, C), C)
        for p in range(n_pairs):
            cols = slice(p * LANES, (p + 1) * LANES)
            bst_ref[i, p] = bstate_ref[p]
            upd = kv_outer(krot_ref[rows, cols], kdecb_ref[p], v_ref[0, rows, cols])
            bstate_ref[p] = cdecb_ref[p] * bstate_ref[p] + upd
        return carry

    lax.fori_loop(0, nc, back_body, 0)

    fstate_ref[...] = jnp.zeros_like(fstate_ref)

    def fwd_body(i, carry):
        rows = pl.ds(pl.multiple_of(i * C, C), C)
        for p in range(n_pairs):
            cols = slice(p * LANES, (p + 1) * LANES)
            qp, kp, vp = qrot_ref[rows, cols], krot_ref[rows, cols], v_ref[0, rows, cols]
            zero = jnp.zeros_like(qp)
            s0 = _dot_nt(jnp.where(first_head, qp, zero), kp) * dmat_ref[2 * p]
            s1 = _dot_nt(jnp.where(first_head, zero, qp), kp) * dmat_ref[2 * p + 1]
            intra = jnp.where(first_head, _dot(s0.astype(BF16), vp), _dot(s1.astype(BF16), vp))
            q32 = qp.astype(F32)
            qd = qdec_ref[p]
            qq = jnp.concatenate([q32 * qd[:, :LANES], q32 * qd[:, LANES:]], axis=1).astype(BF16)
            st = jnp.concatenate([fstate_ref[p], bst_ref[i, p]], axis=0).astype(BF16)
            o = intra + _dot(qq, st)
            o2 = o * o
            s_first = jnp.sum(jnp.where(first_head, o2, 0.0), axis=-1, keepdims=True)
            s_all = jnp.sum(o2, axis=-1, keepdims=True)
            ms = jnp.where(first_head, s_first, s_all - s_first) * (1.0 / dk)
            on = o * lax.rsqrt(ms + EPS) * gn_ref[:, cols]
            g = g_ref[0, rows, cols].astype(F32)
            o_ref[0, rows, cols] = (g * jax.nn.sigmoid(g) * on).astype(BF16)
            upd = kv_outer(kp, kdecf_ref[p], vp)
            fstate_ref[p] = cdecf_ref[p] * fstate_ref[p] + upd
        return carry

    lax.fori_loop(0, nc, fwd_body, 0)


def _retention_tables(p_fwd, p_bwd, C, dk):
    H = p_fwd.shape[0]
    lg_f = -jnp.exp(p_fwd.astype(F32))
    lg_b = -jnp.exp(p_bwd.astype(F32))
    idx = jnp.arange(C, dtype=F32)
    dist = idx[:, None] - idx[None, :]
    dmat = jnp.where(dist >= 0,
                     jnp.exp(lg_f[:, None, None] * jnp.maximum(dist, 0.0)),
                     jnp.exp(lg_b[:, None, None] * jnp.maximum(-dist, 0.0)))

    def lanes(per_head):
        t = jnp.repeat(per_head[:, :, None], dk, axis=2)
        return t.reshape(H // 2, 2, C, dk).transpose(0, 2, 1, 3).reshape(H // 2, C, 2 * dk)

    qdec_f = lanes(jnp.exp(lg_f[:, None] * (idx + 1.0)))
    qdec_b = lanes(jnp.exp(lg_b[:, None] * (C - idx)))
    kdec_f = lanes(jnp.exp(lg_f[:, None] * (C - 1.0 - idx)))
    kdec_b = lanes(jnp.exp(lg_b[:, None] * idx))
    qdec = jnp.concatenate([qdec_f, qdec_b], axis=2)

    def rows(lg):
        per_row = jnp.repeat(jnp.exp(lg * C).reshape(H // 2, 2), dk, axis=1)
        return jnp.broadcast_to(per_row[:, :, None], (H // 2, 2 * dk, 2 * dk))

    return dmat, qdec, kdec_f, kdec_b, rows(lg_f), rows(lg_b)


def _rope_tables(positions, rot_dim, theta, head_dim=64):
    half = rot_dim // 2
    j = jnp.arange(LANES) % head_dim
    inv_freq = jnp.exp(-math.log(theta) * (j % half).astype(F32) * (2.0 / rot_dim))
    ang = positions.astype(F32)[:, :, None] * inv_freq
    cos_t = jnp.where(j < rot_dim, jnp.cos(ang), 1.0)
    sin_t = jnp.where(j < rot_dim, jnp.where(j < half, -jnp.sin(ang), jnp.sin(ang)), 0.0)
    return cos_t, sin_t


def retention(proj, cos_t, sin_t, p_fwd, p_bwd, gn_g, *, col0, width, dk=64):
    B, L, _ = proj.shape
    C = min(RET_CHUNK, L)
    assert L % C == 0 and width % LANES == 0
    n_pairs = width // LANES
    nc = L // C
    tables = _retention_tables(p_fwd, p_bwd, C, dk)
    col = lambda c: pl.BlockSpec((1, L, width), lambda b: (b, 0, col0 + c))
    rope = pl.BlockSpec((1, L, LANES), lambda b: (b, 0, 0))
    full = lambda a: pl.BlockSpec(a.shape, lambda b: (0,) * a.ndim)
    gn = gn_g.reshape(1, width).astype(F32)
    return pl.pallas_call(
        functools.partial(_retention_kernel, chunk=C, dk=dk),
        grid=(B,),
        in_specs=[col(0), col(1), col(2), col(3), rope, rope] + [full(t) for t in tables] + [full(gn)],
        out_specs=pl.BlockSpec((1, L, width), lambda b: (b, 0, 0)),
        out_shape=jax.ShapeDtypeStruct((B, L, width), BF16),
        scratch_shapes=[pltpu.VMEM((L, width), BF16), pltpu.VMEM((L, width), BF16),
                        pltpu.VMEM((nc, n_pairs, LANES, LANES), F32),
                        pltpu.VMEM((n_pairs, LANES, LANES), F32),
                        pltpu.VMEM((n_pairs, LANES, LANES), F32)],
        compiler_params=_params("parallel"),
        name="retention",
    )(proj, proj, proj, proj, cos_t, sin_t, *tables, gn)


def _diff_attn_kernel(lam_ref, q_ref, k_ref, v_ref, cosk_ref, sink_ref, cosq_ref, sinq_ref, gn_ref, ones_ref,
                      o_ref, krot_ref, *, d, rot_half, post_scale):
    n_heads = q_ref.shape[2] // LANES

    @pl.when(pl.program_id(1) == 0)
    def _():
        for h in range(n_heads):
            cols = slice(h * LANES, (h + 1) * LANES)
            kr = _rotate_pairs(k_ref[0, :, cols].astype(F32), cosk_ref[0], sink_ref[0], rot_half)
            krot_ref[:, cols] = kr.astype(BF16)

    lam = lam_ref[0, 0]
    tq = q_ref.shape[1]
    first_map = lax.broadcasted_iota(jnp.int32, (tq, LANES), 1) < d

    def attend(q, kp, v_ones):
        s = _dot_nt(q.astype(BF16), kp)
        e = jnp.exp2(s - jnp.max(s, axis=-1, keepdims=True)).astype(BF16)
        o = _dot(e, v_ones)
        return o[:, :LANES] / o[:, LANES:LANES + 1]

    for h in range(n_heads):
        cols = slice(h * LANES, (h + 1) * LANES)
        qr = _rotate_pairs(q_ref[0, :, cols].astype(F32), cosq_ref[0], sinq_ref[0], rot_half)
        qr = qr * (d ** -0.5 * LOG2E)
        kp = krot_ref[:, cols]
        v_ones = jnp.concatenate([v_ref[0, :, cols], ones_ref[...]], axis=1)
        o = attend(jnp.where(first_map, qr, 0.0), kp, v_ones) - lam * attend(jnp.where(first_map, 0.0, qr), kp, v_ones)
        on = _rms(o, gn_ref[:, cols]) * post_scale
        o_ref[0, :, cols] = on.astype(BF16)


def diff_attention(proj, cos_t, sin_t, lam, gn_g, *, rot_dim, lambda_init, d=64, tq=256):
    B, L, N = proj.shape
    W = N // 3
    tq = min(tq, L)
    assert L % tq == 0
    col = lambda c, rows, f: pl.BlockSpec((1, rows, W), f(c))
    q_map = lambda c: (lambda b, i: (b, i, c))
    kv_map = lambda c: (lambda b, i: (b, 0, c))
    ones_col = jnp.broadcast_to((jnp.arange(LANES) == 0).astype(BF16), (L, LANES))
    return pl.pallas_call(
        functools.partial(_diff_attn_kernel, d=d, rot_half=rot_dim // 2, post_scale=1.0 - lambda_init),
        grid=(B, L // tq),
        in_specs=[pl.BlockSpec(memory_space=pltpu.SMEM),
                  col(0, tq, q_map), col(1, L, kv_map), col(2, L, kv_map),
                  pl.BlockSpec((1, L, LANES), lambda b, i: (b, 0, 0)),
                  pl.BlockSpec((1, L, LANES), lambda b, i: (b, 0, 0)),
                  pl.BlockSpec((1, tq, LANES), lambda b, i: (b, i, 0)),
                  pl.BlockSpec((1, tq, LANES), lambda b, i: (b, i, 0)),
                  pl.BlockSpec((1, W), lambda b, i: (0, 0)),
                  pl.BlockSpec((L, LANES), lambda b, i: (0, 0))],
        out_specs=pl.BlockSpec((1, tq, W), lambda b, i: (b, i, 0)),
        out_shape=jax.ShapeDtypeStruct((B, L, W), BF16),
        scratch_shapes=[pltpu.VMEM((L, W), BF16)],
        compiler_params=_params("parallel", "arbitrary"),
        name="diff_attention",
    )(lam.reshape(1, 1).astype(F32), proj, proj, proj, cos_t, sin_t, cos_t, sin_t,
      gn_g.reshape(1, W).astype(F32), ones_col)


def _mix_xattn_kernel(*refs, n_parts, heads):
    parts, w_out_ref = refs[:n_parts], refs[n_parts]
    x_ref, g_ref, wq_ref, kv_ref, wo_ref, o_ref = refs[n_parts + 1:]
    x = x_ref[0]
    D = x.shape[1]
    hd = D // heads
    row = 0
    for a_ref in parts:
        k = a_ref.shape[2]
        x = x + _dot(a_ref[0], w_out_ref[0, row:row + k, :])
        row += k
    hn = _rms(x, g_ref[...]).astype(BF16)
    q = (_dot(hn, wq_ref[0]) * (hd ** -0.5)).astype(BF16)
    outs = []
    for h in range(heads):
        kh = kv_ref[0, :, h * hd:(h + 1) * hd]
        vh = kv_ref[0, :, D + h * hd:D + (h + 1) * hd]
        s = _dot_nt(q[:, h * hd:(h + 1) * hd], kh)
        e = jnp.exp(s - jnp.max(s, axis=-1, keepdims=True))
        oh = _dot(e.astype(BF16), vh) / jnp.sum(e, axis=-1, keepdims=True)
        outs.append(oh.astype(BF16))
    o_ref[0] = x + _dot(jnp.concatenate(outs, axis=-1), wo_ref[0])


def mixer_out_cross_attention(x, parts, w_out, mix_layer, g, wq, kv, wo, layer, *, tq=512):
    B, L, D = x.shape
    M = kv.shape[1]
    tq = min(tq, L)
    assert L % tq == 0
    tile = lambda width: pl.BlockSpec((1, tq, width), lambda b, i: (b, i, 0))
    return pl.pallas_call(
        functools.partial(_mix_xattn_kernel, n_parts=len(parts), heads=XA_HEADS),
        grid=(B, L // tq),
        in_specs=[tile(a.shape[2]) for a in parts] + [
            _layer_spec(w_out, mix_layer), tile(D), pl.BlockSpec((1, D), lambda b, i: (0, 0)),
            _layer_spec(wq, layer), pl.BlockSpec((1, M, 2 * D), lambda b, i: (b, 0, 0)),
            _layer_spec(wo, layer)],
        out_specs=tile(D),
        out_shape=jax.ShapeDtypeStruct((B, L, D), F32),
        compiler_params=_params("parallel", "parallel"),
        name="mixer_out_cross_attention",
    )(*parts, w_out, x, g.reshape(1, D), wq, kv, wo)


def _swiglu_kernel(x_ref, g_ref, wg_ref, wu_ref, wd_ref, o_ref, *, tf):
    x = x_ref[...]
    hn = _rms(x, g_ref[...]).astype(BF16)
    acc = x
    for c in range(wg_ref.shape[2] // tf):
        cols = slice(c * tf, (c + 1) * tf)
        a = _dot(hn, wg_ref[0, :, cols])
        b = _dot(hn, wu_ref[0, :, cols])
        acc = acc + _dot((a * jax.nn.sigmoid(a) * b).astype(BF16), wd_ref[0, cols, :])
    o_ref[...] = acc


def swiglu_residual(x, g, w_gate, w_up, w_down, layer, *, tm=512, tf=256):
    T, D = x.shape
    FF = w_gate.shape[2]
    tm, tf = min(tm, T), min(tf, FF)
    assert T % tm == 0 and FF % tf == 0
    return pl.pallas_call(
        functools.partial(_swiglu_kernel, tf=tf),
        grid=(T // tm,),
        in_specs=[pl.BlockSpec((tm, D), lambda i: (i, 0)),
                  pl.BlockSpec((1, D), lambda i: (0, 0)),
                  _layer_spec(w_gate, layer), _layer_spec(w_up, layer), _layer_spec(w_down, layer)],
        out_specs=pl.BlockSpec((tm, D), lambda i: (i, 0)),
        out_shape=jax.ShapeDtypeStruct((T, D), F32),
        compiler_params=_params("parallel"),
        name="swiglu_residual",
    )(x, g.reshape(1, D), w_gate, w_up, w_down)


SLOT_TILE = 1024
DISPATCH_TILE = 256
DISPATCH_CHUNK = 1024
COMBINE_TILE = 512
COMBINE_PIECE = 128
ROUTE_ROWS = 8
_E1, _E2, _W1, _W2, _RANK1, _RANK2 = range(6)
_POS1, _POS2, _GATE1, _GATE2 = range(4)


def _router_kernel(x_ref, g_ref, wr_ref, hn_ref, meta_ref, cum_ref, counts_ref, run_ref, *, n_experts):
    @pl.when(pl.program_id(0) == 0)
    def _():
        run_ref[...] = jnp.zeros_like(run_ref)

    hn = _rms(x_ref[...], g_ref[...])
    hn_ref[...] = hn.astype(BF16)
    logits = jnp.dot(hn, wr_ref[...], preferred_element_type=F32, precision=lax.Precision.HIGHEST)
    tm = logits.shape[0]
    lane = lax.broadcasted_iota(jnp.int32, logits.shape, 1)
    neg = jnp.float32(-jnp.inf)
    logits = jnp.where(lane < n_experts, logits, neg)
    m1 = jnp.max(logits, axis=-1, keepdims=True)
    i1 = jnp.min(jnp.where(logits == m1, lane, LANES), axis=-1, keepdims=True)
    rest = jnp.where(lane == i1, neg, logits)
    m2 = jnp.max(rest, axis=-1, keepdims=True)
    i2 = jnp.min(jnp.where(rest == m2, lane, LANES), axis=-1, keepdims=True)
    e2 = jnp.exp(m2 - m1)
    w1 = 1.0 / (1.0 + e2)
    w2 = e2 / (1.0 + e2)

    chosen = jnp.where((lane == i1) | (lane == i2), 1.0, 0.0)
    r_i = lax.broadcasted_iota(jnp.int32, (tm, tm), 0)
    c_i = lax.broadcasted_iota(jnp.int32, (tm, tm), 1)
    tri = jnp.where(c_i < r_i, 1.0, 0.0).astype(BF16)
    cum = _dot(tri, chosen.astype(BF16)) + run_ref[...]
    cum_ref[...] = cum
    rank1 = jnp.sum(jnp.where(lane == i1, cum, 0.0), axis=-1, keepdims=True)
    rank2 = jnp.sum(jnp.where(lane == i2, cum, 0.0), axis=-1, keepdims=True)
    run_ref[...] += jnp.sum(chosen, axis=0, keepdims=True)
    counts_ref[...] = run_ref[...]

    meta = jnp.zeros_like(logits)
    for idx, val in ((_E1, i1.astype(F32)), (_E2, i2.astype(F32)), (_W1, w1), (_W2, w2),
                     (_RANK1, rank1), (_RANK2, rank2)):
        meta = jnp.where(lane == idx, val, meta)
    meta_ref[...] = meta


def route_tokens(x, g, router_w, *, tm=512):
    T, D = x.shape
    E = router_w.shape[1]
    tm = min(tm, T)
    wr = jnp.pad(router_w.astype(F32), ((0, 0), (0, LANES - E)))
    row = pl.BlockSpec((tm, LANES), lambda i: (i, 0))
    return pl.pallas_call(
        functools.partial(_router_kernel, n_experts=E),
        grid=(T // tm,),
        in_specs=[pl.BlockSpec((tm, D), lambda i: (i, 0)),
                  pl.BlockSpec((1, D), lambda i: (0, 0)),
                  pl.BlockSpec((D, LANES), lambda i: (0, 0))],
        out_specs=[pl.BlockSpec((tm, D), lambda i: (i, 0)), row, row,
                   pl.BlockSpec((1, LANES), lambda i: (0, 0))],
        out_shape=[jax.ShapeDtypeStruct((T, D), BF16), jax.ShapeDtypeStruct((T, LANES), F32),
                   jax.ShapeDtypeStruct((T, LANES), F32), jax.ShapeDtypeStruct((1, LANES), F32)],
        scratch_shapes=[pltpu.VMEM((1, LANES), F32)],
        compiler_params=_params("arbitrary"),
        name="route_tokens",
    )(x, g.reshape(1, D), wr)


def _slots_kernel(off_ref, meta_ref, rec_ref, rows_ref, *, n_experts):
    meta = meta_ref[...]
    e1, e2 = meta[:, _E1:_E1 + 1], meta[:, _E2:_E2 + 1]
    off1, off2 = jnp.zeros_like(e1), jnp.zeros_like(e2)
    for e in range(n_experts):
        off = off_ref[e].astype(F32)
        off1 = jnp.where(e1 == e, off, off1)
        off2 = jnp.where(e2 == e, off, off2)
    lane = lax.broadcasted_iota(jnp.int32, meta.shape, 1)
    rec = jnp.zeros_like(meta)
    for idx, val in ((_POS1, off1 + meta[:, _RANK1:_RANK1 + 1]), (_POS2, off2 + meta[:, _RANK2:_RANK2 + 1]),
                     (_GATE1, meta[:, _W1:_W1 + 1]), (_GATE2, meta[:, _W2:_W2 + 1])):
        rec = jnp.where(lane == idx, val, rec)
    rec_ref[...] = rec
    rows_ref[...] = rec.T[:ROUTE_ROWS, :]


def assign_slots(meta, group_offset, *, n_experts, tm=512):
    T = meta.shape[0]
    tm = min(tm, T)
    return pl.pallas_call(
        functools.partial(_slots_kernel, n_experts=n_experts),
        grid_spec=pltpu.PrefetchScalarGridSpec(
            num_scalar_prefetch=1,
            grid=(T // tm,),
            in_specs=[pl.BlockSpec((tm, LANES), lambda i, off: (i, 0))],
            out_specs=[pl.BlockSpec((tm, LANES), lambda i, off: (i, 0)),
                       pl.BlockSpec((ROUTE_ROWS, tm), lambda i, off: (0, i))],
        ),
        out_shape=[jax.ShapeDtypeStruct((T, LANES), F32), jax.ShapeDtypeStruct((ROUTE_ROWS, T), F32)],
        compiler_params=_params("parallel"),
        name="assign_slots",
    )(group_offset, meta)


def _flatten_items(n_items, n_slots):
    i32 = jnp.int32
    end = jnp.cumsum(n_items).astype(i32)
    start = end - n_items
    total = end[-1]
    w = jnp.arange(n_slots, dtype=i32)
    wc = jnp.minimum(w, total - 1)
    owner = jnp.minimum(jnp.sum(wc[:, None] >= end[None, :], axis=1), n_items.shape[0] - 1).astype(i32)
    return owner, wc - start[owner], w < total, wc, start


def _moe_plan(cum, counts, *, n_experts, expert_base):
    i32 = jnp.int32
    T, E = cum.shape[0], n_experts
    cnt = counts[0, :E].astype(i32)
    tiles_e = (cnt + SLOT_TILE - 1) // SLOT_TILE
    tile_end_e = jnp.cumsum(tiles_e).astype(i32)
    tile_start_e = tile_end_e - tiles_e
    n_used = tile_end_e[-1]
    off_e = tile_start_e * SLOT_TILE

    n_tiles = 2 * T // SLOT_TILE + E
    r = jnp.arange(n_tiles, dtype=i32)
    tile_expert = jnp.minimum(jnp.sum(r[:, None] >= tile_end_e[None, :], axis=1), E - 1).astype(i32)

    per = SLOT_TILE // DISPATCH_TILE
    d = jnp.arange(n_tiles * per, dtype=i32)
    e_d = tile_expert[d // per]
    used_d = (d // per) < n_used
    a_d = (d - tile_start_e[e_d] * per) * DISPATCH_TILE
    b_d = jnp.minimum(a_d + DISPATCH_TILE, cnt[e_d])
    filled_d = b_d > a_d
    cum_incl = jnp.concatenate([cum[1:, :E], counts[:, :E]], axis=0).astype(i32)
    col = cum_incl.T[e_d]
    t_lo = jnp.sum(col <= a_d[:, None], axis=1)
    t_hi = jnp.sum(col <= (b_d - 1)[:, None], axis=1)
    c_lo = jnp.where(filled_d, t_lo // DISPATCH_CHUNK, 0)
    c_hi = jnp.where(filled_d, t_hi // DISPATCH_CHUNK, 0)
    n_disp = jnp.where(used_d, c_hi - c_lo + 1, 0).astype(i32)
    owner, k, real, wc, start = _flatten_items(n_disp, E * (T // DISPATCH_CHUNK) + n_tiles * per)
    disp = (owner, (c_lo[owner] + k).astype(i32),
            jnp.where(real, jnp.where(k == 0, 1, 2), 0).astype(i32))

    n_tok_tiles = T // COMBINE_TILE
    cum_b = cum[::COMBINE_TILE, :E].astype(i32)
    cum_n = jnp.concatenate([cum_b[1:], cnt[None, :]], axis=0)
    s_a, s_b = off_e[None, :] + cum_b, off_e[None, :] + cum_n
    p_lo, p_hi = s_a // COMBINE_PIECE, (s_b - 1) // COMBINE_PIECE
    n_piece = jnp.where(s_b > s_a, p_hi - p_lo + 1, 0).astype(i32)
    piece_end = jnp.cumsum(n_piece, axis=1).astype(i32)
    piece_start = piece_end - n_piece
    n_valid = piece_end[:, -1]
    kk = jnp.minimum(jnp.arange(_combine_pieces(E), dtype=i32)[None, :], n_valid[:, None] - 1)
    owner = jnp.sum(kk[:, :, None] >= piece_end[:, None, :], axis=2).astype(i32)
    is_owner = owner[:, :, None] == jnp.arange(E, dtype=i32)[None, None, :]
    take = lambda a: jnp.sum(jnp.where(is_owner, a[:, None, :], 0), axis=2)
    piece = take(p_lo) + kk - take(piece_start)
    comb = (piece.reshape(-1).astype(i32), n_valid)

    return dict(group_offset=off_e, disp=disp, comb=comb, tile_expert=tile_expert + expert_base,
                n_used=n_used.reshape(1), n_tiles=n_tiles)


def _dispatch_kernel(tile_ref, chunk_ref, flag_ref, rows_ref, hn_ref, o_ref):
    w = pl.program_id(0)
    flag = flag_ref[w]
    rows, cols = o_ref.shape[0], hn_ref.shape[0]

    def gathered():
        slot = (tile_ref[w] * rows + lax.broadcasted_iota(jnp.int32, (rows, cols), 0)).astype(F32)
        hit = (rows_ref[_POS1:_POS1 + 1, :] == slot) | (rows_ref[_POS2:_POS2 + 1, :] == slot)
        return _dot(jnp.where(hit, 1.0, 0.0).astype(BF16), hn_ref[...]).astype(BF16)

    @pl.when(flag == 1)
    def _():
        o_ref[...] = gathered()

    @pl.when(flag == 2)
    def _():
        o_ref[...] += gathered()


def dispatch_tokens(hn, rows, plan):
    T, D = hn.shape
    tile, chunk, flag = plan["disp"]
    return pl.pallas_call(
        _dispatch_kernel,
        grid_spec=pltpu.PrefetchScalarGridSpec(
            num_scalar_prefetch=3,
            grid=(tile.shape[0],),
            in_specs=[pl.BlockSpec((ROUTE_ROWS, DISPATCH_CHUNK), lambda w, t, c, f: (0, c[w])),
                      pl.BlockSpec((DISPATCH_CHUNK, D), lambda w, t, c, f: (c[w], 0))],
            out_specs=pl.BlockSpec((DISPATCH_TILE, D), lambda w, t, c, f: (t[w], 0)),
        ),
        out_shape=jax.ShapeDtypeStruct((plan["n_tiles"] * SLOT_TILE, D), BF16),
        compiler_params=_params("arbitrary"),
        name="moe_dispatch",
    )(tile, chunk, flag, rows, hn)


def _expert_kernel(te_ref, nu_ref, x_ref, wg_ref, wu_ref, wd_ref, o_ref, acc_ref, *, sub):
    r, f = pl.program_id(0), pl.program_id(1)

    @pl.when(f == 0)
    def _():
        acc_ref[...] = jnp.zeros_like(acc_ref)

    @pl.when(r < nu_ref[0])
    def _():
        x = x_ref[...]
        acc = acc_ref[...]
        for c in range(wg_ref.shape[2] // sub):
            cols = slice(c * sub, (c + 1) * sub)
            a = _dot(x, wg_ref[0, :, cols])
            b = _dot(x, wu_ref[0, :, cols])
            acc = acc + _dot((a * jax.nn.sigmoid(a) * b).astype(BF16), wd_ref[0, cols, :])
        acc_ref[...] = acc

    @pl.when(f == pl.num_programs(1) - 1)
    def _():
        o_ref[...] = acc_ref[...].astype(BF16)


def expert_swiglu(xs, plan, w_gate, w_up, w_down, *, tf=1792, sub=256):
    S, D = xs.shape
    FF = w_gate.shape[2]
    tf = tf if FF % tf == 0 else FF
    sub = min(sub, tf)
    assert FF % tf == 0 and tf % sub == 0
    nf = FF // tf
    last = lambda r, nu: jnp.minimum(r, nu[0] - 1)
    col = lambda r, f, nu: jnp.where(r < nu[0], f, nf - 1)
    return pl.pallas_call(
        functools.partial(_expert_kernel, sub=sub),
        grid_spec=pltpu.PrefetchScalarGridSpec(
            num_scalar_prefetch=2,
            grid=(S // SLOT_TILE, nf),
            in_specs=[pl.BlockSpec((SLOT_TILE, D), lambda r, f, te, nu: (last(r, nu), 0)),
                      pl.BlockSpec((1, D, tf), lambda r, f, te, nu: (te[last(r, nu)], 0, col(r, f, nu))),
                      pl.BlockSpec((1, D, tf), lambda r, f, te, nu: (te[last(r, nu)], 0, col(r, f, nu))),
                      pl.BlockSpec((1, tf, D), lambda r, f, te, nu: (te[last(r, nu)], col(r, f, nu), 0))],
            out_specs=pl.BlockSpec((SLOT_TILE, D), lambda r, f, te, nu: (r, 0)),
            scratch_shapes=[pltpu.VMEM((SLOT_TILE, D), F32)],
        ),
        out_shape=jax.ShapeDtypeStruct((S, D), BF16),
        compiler_params=_params("arbitrary", "arbitrary"),
        name="moe_experts",
    )(plan["tile_expert"], plan["n_used"], xs, w_gate, w_up, w_down)


def _combine_pieces(n_experts):
    return 2 * COMBINE_TILE // COMBINE_PIECE + 2 * n_experts


def _combine_kernel(piece_ref, nv_ref, rec_ref, x_ref, g_ref, *rest, n_pieces, normalize):
    y_refs, o_ref = rest[:n_pieces], rest[n_pieces]
    i = pl.program_id(0)
    rows = o_ref.shape[0]
    rec = rec_ref[...]
    pos1, pos2 = rec[:, _POS1:_POS1 + 1], rec[:, _POS2:_POS2 + 1]
    gate1, gate2 = rec[:, _GATE1:_GATE1 + 1], rec[:, _GATE2:_GATE2 + 1]
    lane = lax.broadcasted_iota(jnp.int32, (rows, COMBINE_PIECE), 1)

    def weights(k):
        base = jnp.where(k < nv_ref[i], piece_ref[i * n_pieces + k] * COMBINE_PIECE, -COMBINE_PIECE)
        slot = (base + lane).astype(F32)
        return (jnp.where(pos1 == slot, gate1, 0.0) + jnp.where(pos2 == slot, gate2, 0.0)).astype(BF16)

    o_ref[...] = x_ref[...]
    for k in range(0, n_pieces, 2):
        @pl.when(k < nv_ref[i])
        def _():
            sel = jnp.concatenate([weights(k), weights(k + 1)], axis=1)
            y = jnp.concatenate([y_refs[k][...], y_refs[k + 1][...]], axis=0)
            o_ref[...] += _dot(sel, y)

    if normalize:
        o_ref[...] = _rms(o_ref[...], g_ref[...])


def combine_tokens(x, y, rec, plan, *, n_experts, final_g=None):
    T, D = x.shape
    piece, n_valid = plan["comb"]
    n_pieces = _combine_pieces(n_experts)
    g = jnp.ones((1, D), F32) if final_g is None else final_g.reshape(1, D).astype(F32)
    y_spec = lambda k: pl.BlockSpec((COMBINE_PIECE, D), lambda i, p, nv: (p[i * n_pieces + k], 0))
    return pl.pallas_call(
        functools.partial(_combine_kernel, n_pieces=n_pieces, normalize=final_g is not None),
        grid_spec=pltpu.PrefetchScalarGridSpec(
            num_scalar_prefetch=2,
            grid=(T // COMBINE_TILE,),
            in_specs=[pl.BlockSpec((COMBINE_TILE, LANES), lambda i, p, nv: (i, 0)),
                      pl.BlockSpec((COMBINE_TILE, D), lambda i, p, nv: (i, 0)),
                      pl.BlockSpec((1, D), lambda i, p, nv: (0, 0))] + [y_spec(k) for k in range(n_pieces)],
            out_specs=pl.BlockSpec((COMBINE_TILE, D), lambda i, p, nv: (i, 0)),
        ),
        out_shape=jax.ShapeDtypeStruct((T, D), F32),
        compiler_params=_params("parallel"),
        name="moe_combine",
    )(piece, n_valid, rec, x, g, *([y] * n_pieces))


def moe_residual(x, g, router_w, w_gate, w_up, w_down, layer, *, final_g=None):
    E = router_w.shape[1]
    hn, meta, cum, counts = route_tokens(x, g, router_w)
    plan = _moe_plan(cum, counts, n_experts=E, expert_base=layer * E)
    rec, rows = assign_slots(meta, plan["group_offset"], n_experts=E)
    xs = dispatch_tokens(hn, rows, plan)
    y = expert_swiglu(xs, plan, w_gate, w_up, w_down)
    return combine_tokens(x, y, rec, plan, n_experts=E, final_g=final_g)


def _final_norm_kernel(x_ref, g_ref, o_ref):
    o_ref[...] = _rms(x_ref[...], g_ref[...])


def final_norm(x, g, *, tm=1024):
    T, D = x.shape
    tm = min(tm, T)
    return pl.pallas_call(
        _final_norm_kernel,
        grid=(T // tm,),
        in_specs=[pl.BlockSpec((tm, D), lambda i: (i, 0)), pl.BlockSpec((1, D), lambda i: (0, 0))],
        out_specs=pl.BlockSpec((tm, D), lambda i: (i, 0)),
        out_shape=jax.ShapeDtypeStruct((T, D), F32),
        compiler_params=_params("parallel"),
        name="final_norm",
    )(x, g.reshape(1, D))


def kernel(x, mem, positions, mem_norm_g, final_norm_g, mix_norm_g, xa_norm_g, ffn_norm_g, xa_wq, xa_wkv, xa_wo, ev_w_in, ev_conv_w, ev_ret_decay_f, ev_ret_decay_b, ev_ret_gn_g, ev_w_out, ffd_w_gate, ffd_w_up, ffd_w_down, od_w_in, od_lam_q1, od_lam_k1, od_lam_q2, od_lam_k2, od_gn_g, od_w_out, moe_router, moe_w_gate, moe_w_up, moe_w_down):
    B, L, D = x.shape
    M = mem.shape[1]
    T = B * L
    depth = mix_norm_g.shape[0]
    conv_ch = ev_conv_w.shape[2]
    ret_w = ev_ret_gn_g.shape[1]
    ret_dk = ret_w // RET_HEADS
    diff_d = od_lam_q1.shape[1]
    bf = lambda w: w.astype(BF16)
    experts = lambda w: bf(w).reshape((-1,) + w.shape[2:])

    xa_wq, xa_wkv, xa_wo = bf(xa_wq), bf(xa_wkv), bf(xa_wo)
    ev_w_in, ev_w_out, od_w_in, od_w_out = bf(ev_w_in), bf(ev_w_out), bf(od_w_in), bf(od_w_out)
    ffd_w_gate, ffd_w_up, ffd_w_down = bf(ffd_w_gate), bf(ffd_w_up), bf(ffd_w_down)
    moe_w_gate, moe_w_up, moe_w_down = experts(moe_w_gate), experts(moe_w_up), experts(moe_w_down)
    ev_conv_w = ev_conv_w.astype(F32)

    ret_cos, ret_sin = _rope_tables(positions, ret_dk, RET_THETA)
    diff_cos, diff_sin = _rope_tables(positions, diff_d // 4, ROPE_THETA)
    mem2 = mem.reshape(B * M, D)
    xt = x.reshape(T, D)

    for i in range(depth):
        j = i // 2
        if i % 2 == 0:
            proj = norm_proj(xt, mix_norm_g[i], ev_w_in, j).reshape(B, L, -1)
            a = short_conv(proj, ev_conv_w, j, ch=conv_ch)
            r = retention(proj, ret_cos, ret_sin, ev_ret_decay_f[j], ev_ret_decay_b[j], ev_ret_gn_g[j],
                          col0=3 * conv_ch // ret_w, width=ret_w, dk=ret_dk)
            parts, w_out = [a, r], ev_w_out
        else:
            lambda_init = 0.8 - 0.6 * math.exp(-0.3 * i)
            f32 = lambda v: v.astype(F32)
            lam = (jnp.exp(jnp.sum(f32(od_lam_q1[j]) * f32(od_lam_k1[j])))
                   - jnp.exp(jnp.sum(f32(od_lam_q2[j]) * f32(od_lam_k2[j]))) + lambda_init)
            proj = norm_proj(xt, mix_norm_g[i], od_w_in, j).reshape(B, L, -1)
            o = diff_attention(proj, diff_cos, diff_sin, lam, od_gn_g[j], rot_dim=diff_d // 4,
                               lambda_init=lambda_init, d=diff_d)
            parts, w_out = [o], od_w_out

        kv = norm_proj(mem2, mem_norm_g, xa_wkv, i).reshape(B, M, 2 * D)
        xt = mixer_out_cross_attention(xt.reshape(B, L, D), parts, w_out, j, xa_norm_g[i], xa_wq, kv, xa_wo,
                                       i).reshape(T, D)

        if i % 2 == 0:
            xt = swiglu_residual(xt, ffn_norm_g[i], ffd_w_gate, ffd_w_up, ffd_w_down, j)
        else:
            xt = moe_residual(xt, ffn_norm_g[i], moe_router[j], moe_w_gate, moe_w_up, moe_w_down, j,
                              final_g=final_norm_g if i == depth - 1 else None)

    if depth % 2 == 1:
        xt = final_norm(xt, final_norm_g)
    return xt.reshape(B, L, D)
```

```python
import functools
import math

import jax
import jax.numpy as jnp
from jax import lax
from jax.experimental import pallas as pl
from jax.experimental.pallas import tpu as pltpu

F32 = jnp.float32
BF16 = jnp.bfloat16
EPS = 1e-6
LANES = 128
LOG2E = 1.4426950408889634

RET_HEADS = 8
RET_THETA = 10000.0
RET_CHUNK = 256
ROPE_THETA = 500000.0
XA_HEADS = 4

VMEM_LIMIT = 56 * 1024 * 1024


def _params(*sem, vmem_limit=VMEM_LIMIT):
    return pltpu.CompilerParams(dimension_semantics=sem, vmem_limit_bytes=vmem_limit)


def _rms(x, g):
    ms = jnp.mean(x * x, axis=-1, keepdims=True)
    return x * lax.rsqrt(ms + EPS) * g


def _dot(a, b):
    return jnp.dot(a, b, preferred_element_type=F32)


def _dot_nt(a, b):
    return lax.dot_general(a, b, (((1,), (1,)), ((), ())), preferred_element_type=F32)


def _rotate_pairs(x, cos_t, sin_t, half):
    lane = lax.broadcasted_iota(jnp.int32, x.shape, 1) % 64
    partner = jnp.where(lane < half, pltpu.roll(x, LANES - half, axis=1), pltpu.roll(x, half, axis=1))
    return x * cos_t + partner * sin_t


def _layer_spec(w, layer):
    return pl.BlockSpec((1,) + w.shape[1:], lambda *_: (layer, 0, 0))


def _norm_proj_kernel(x_ref, g_ref, w_ref, o_ref, *, tn):
    hn = _rms(x_ref[...], g_ref[...]).astype(BF16)
    for c in range(o_ref.shape[1] // tn):
        cols = slice(c * tn, (c + 1) * tn)
        o_ref[:, cols] = _dot(hn, w_ref[0, :, cols]).astype(o_ref.dtype)


def norm_proj(x, g, w, layer, *, tm=1024, tn=512):
    T, D = x.shape
    N = w.shape[2]
    tm, tn = min(tm, T), min(tn, N)
    assert T % tm == 0 and N % tn == 0
    return pl.pallas_call(
        functools.partial(_norm_proj_kernel, tn=tn),
        grid=(T // tm,),
        in_specs=[pl.BlockSpec((tm, D), lambda i: (i, 0)),
                  pl.BlockSpec((1, D), lambda i: (0, 0)),
                  _layer_spec(w, layer)],
        out_specs=pl.BlockSpec((tm, N), lambda i: (i, 0)),
        out_shape=jax.ShapeDtypeStruct((T, N), BF16),
        compiler_params=_params("parallel"),
        name="norm_proj",
    )(x, g.reshape(1, D), w)


def _conv_kernel(gb_ref, gc_ref, h_ref, w_ref, o_ref):
    u = gc_ref[0].astype(F32) * h_ref[0].astype(F32)
    L = u.shape[0]
    row = lax.broadcasted_iota(jnp.int32, u.shape, 0)
    prev = jnp.where(row == 0, 0.0, pltpu.roll(u, 1, axis=0))
    nxt = jnp.where(row == L - 1, 0.0, pltpu.roll(u, L - 1, axis=0))
    conv = w_ref[0, 0:1, :] * prev + w_ref[0, 1:2, :] * u + w_ref[0, 2:3, :] * nxt
    o_ref[0] = (gb_ref[0].astype(F32) * conv).astype(BF16)


def short_conv(proj, conv_w, layer, *, ch):
    B, L, _ = proj.shape
    col = lambda c: pl.BlockSpec((1, L, ch), lambda b: (b, 0, c))
    return pl.pallas_call(
        _conv_kernel,
        grid=(B,),
        in_specs=[col(0), col(1), col(2), _layer_spec(conv_w, layer)],
        out_specs=pl.BlockSpec((1, L, ch), lambda b: (b, 0, 0)),
        out_shape=jax.ShapeDtypeStruct((B, L, ch), BF16),
        compiler_params=_params("parallel"),
        name="short_conv",
    )(proj, proj, proj, conv_w)


def _retention_kernel(q_ref, k_ref, v_ref, g_ref, cos_ref, sin_ref, dmat_ref, qdec_ref, kdecf_ref,
                      kdecb_ref, cdecf_ref, cdecb_ref, gn_ref, o_ref, qrot_ref, krot_ref, bst_ref,
                      fstate_ref, bstate_ref, *, chunk, dk):
    L = q_ref.shape[1]
    n_pairs = q_ref.shape[2] // LANES
    nc = L // chunk
    C = chunk
    cos_t, sin_t = cos_ref[0], sin_ref[0]

    for p in range(n_pairs):
        cols = slice(p * LANES, (p + 1) * LANES)
        qrot_ref[:, cols] = _rotate_pairs(q_ref[0, :, cols].astype(F32), cos_t, sin_t, dk // 2).astype(BF16)
        kr = _rotate_pairs(k_ref[0, :, cols].astype(F32), cos_t, sin_t, dk // 2)
        krot_ref[:, cols] = (kr * (dk ** -0.5)).astype(BF16)

    lane_c = lax.broadcasted_iota(jnp.int32, (C, LANES), 1)
    first_head = lane_c < dk
    r_i = lax.broadcasted_iota(jnp.int32, (LANES, LANES), 0) // dk
    c_i = lax.broadcasted_iota(jnp.int32, (LANES, LANES), 1) // dk
    same_head = r_i == c_i

    def kv_outer(kp, kdec, vp):
        kd_t = (kp.astype(F32) * kdec).T.astype(BF16)
        return jnp.where(same_head, _dot(kd_t, vp), 0.0)

    bstate_ref[...] = jnp.zeros_like(bstate_ref)

    def back_body(t, carry):
        i = nc - 1 - t
        rows = pl.ds(pl.multiple_of(i * C, C), C)
        for p in range(n_pairs):
            cols = slice(p * LANES, (p + 1) * LANES)
            bst_ref[i, p] = bstate_ref[p]
            upd = kv_outer(krot_ref[rows, cols], kdecb_ref[p], v_ref[0, rows, cols])
            bstate_ref[p] = cdecb_ref[p] * bstate_ref[p] + upd
        return carry

    lax.fori_loop(0, nc, back_body, 0)

    fstate_ref[...] = jnp.zeros_like(fstate_ref)

    def fwd_body(i, carry):
        rows = pl.ds(pl.multiple_of(i * C, C), C)
        for p in range(n_pairs):
            cols = slice(p * LANES, (p + 1) * LANES)
            qp, kp, vp = qrot_ref[rows, cols], krot_ref[rows, cols], v_ref[0, rows, cols]
            zero = jnp.zeros_like(qp)
            s0 = _dot_nt(jnp.where(first_head, qp, zero), kp) * dmat_ref[2 * p]
            s1 = _dot_nt(jnp.where(first_head, zero, qp), kp) * dmat_ref[2 * p + 1]
            intra = jnp.where(first_head, _dot(s0.astype(BF16), vp), _dot(s1.astype(BF16), vp))
            q32 = qp.astype(F32)
            qd = qdec_ref[p]
            qq = jnp.concatenate([q32 * qd[:, :LANES], q32 * qd[:, LANES:]], axis=1).astype(BF16)
            st = jnp.concatenate([fstate_ref[p], bst_ref[i, p]], axis=0).astype(BF16)
            o = intra + _dot(qq, st)
            o2 = o * o
            s_first = jnp.sum(jnp.where(first_head, o2, 0.0), axis=-1, keepdims=True)
            s_all = jnp.sum(o2, axis=-1, keepdims=True)
            ms = jnp.where(first_head, s_first, s_all - s_first) * (1.0 / dk)
            on = o * lax.rsqrt(ms + EPS) * gn_ref[:, cols]
            g = g_ref[0, rows, cols].astype(F32)
            o_ref[0, rows, cols] = (g * jax.nn.sigmoid(g) * on).astype(BF16)
            upd = kv_outer(kp, kdecf_ref[p], vp)
            fstate_ref[p] = cdecf_ref[p] * fstate_ref[p] + upd
        return carry

    lax.fori_loop(0, nc, fwd_body, 0)


def _retention_tables(p_fwd, p_bwd, C, dk):
    H = p_fwd.shape[0]
    lg_f = -jnp.exp(p_fwd.astype(F32))
    lg_b = -jnp.exp(p_bwd.astype(F32))
    idx = jnp.arange(C, dtype=F32)
    dist = idx[:, None] - idx[None, :]
    dmat = jnp.where(dist >= 0,
                     jnp.exp(lg_f[:, None, None] * jnp.maximum(dist, 0.0)),
                     jnp.exp(lg_b[:, None, None] * jnp.maximum(-dist, 0.0)))

    def lanes(per_head):
        t = jnp.repeat(per_head[:, :, None], dk, axis=2)
        return t.reshape(H // 2, 2, C, dk).transpose(0, 2, 1, 3).reshape(H // 2, C, 2 * dk)

    qdec_f = lanes(jnp.exp(lg_f[:, None] * (idx + 1.0)))
    qdec_b = lanes(jnp.exp(lg_b[:, None] * (C - idx)))
    kdec_f = lanes(jnp.exp(lg_f[:, None] * (C - 1.0 - idx)))
    kdec_b = lanes(jnp.exp(lg_b[:, None] * idx))
    qdec = jnp.concatenate([qdec_f, qdec_b], axis=2)

    def rows(lg):
        per_row = jnp.repeat(jnp.exp(lg * C).reshape(H // 2, 2), dk, axis=1)
        return jnp.broadcast_to(per_row[:, :, None], (H // 2, 2 * dk, 2 * dk))

    return dmat, qdec, kdec_f, kdec_b, rows(lg_f), rows(lg_b)


def _rope_tables(positions, rot_dim, theta, head_dim=64):
    half = rot_dim // 2
    j = jnp.arange(LANES) % head_dim
    inv_freq = jnp.exp(-math.log(theta) * (j % half).astype(F32) * (2.0 / rot_dim))
    ang = positions.astype(F32)[:, :, None] * inv_freq
    cos_t = jnp.where(j < rot_dim, jnp.cos(ang), 1.0)
    sin_t = jnp.where(j < rot_dim, jnp.where(j < half, -jnp.sin(ang), jnp.sin(ang)), 0.0)
    return cos_t, sin_t


def retention(proj, cos_t, sin_t, p_fwd, p_bwd, gn_g, *, col0, width, dk=64):
    B, L, _ = proj.shape
    C = min(RET_CHUNK, L)
    assert L % C == 0 and width % LANES == 0
    n_pairs = width // LANES
    nc = L // C
    tables = _retention_tables(p_fwd, p_bwd, C, dk)
    col = lambda c: pl.BlockSpec((1, L, width), lambda b: (b, 0, col0 + c))
    rope = pl.BlockSpec((1, L, LANES), lambda b: (b, 0, 0))
    full = lambda a: pl.BlockSpec(a.shape, lambda b: (0,) * a.ndim)
    gn = gn_g.reshape(1, width).astype(F32)
    return pl.pallas_call(
        functools.partial(_retention_kernel, chunk=C, dk=dk),
        grid=(B,),
        in_specs=[col(0), col(1), col(2), col(3), rope, rope] + [full(t) for t in tables] + [full(gn)],
        out_specs=pl.BlockSpec((1, L, width), lambda b: (b, 0, 0)),
        out_shape=jax.ShapeDtypeStruct((B, L, width), BF16),
        scratch_shapes=[pltpu.VMEM((L, width), BF16), pltpu.VMEM((L, width), BF16),
                        pltpu.VMEM((nc, n_pairs, LANES, LANES), F32),
                        pltpu.VMEM((n_pairs, LANES, LANES), F32),
                        pltpu.VMEM((n_pairs, LANES, LANES), F32)],
        compiler_params=_params("parallel"),
        name="retention",
    )(proj, proj, proj, proj, cos_t, sin_t, *tables, gn)


def _diff_attn_kernel(lam_ref, q_ref, k_ref, v_ref, cosk_ref, sink_ref, cosq_ref, sinq_ref, gn_ref, ones_ref,
                      o_ref, krot_ref, *, d, rot_half, post_scale):
    n_heads = q_ref.shape[2] // LANES

    @pl.when(pl.program_id(1) == 0)
    def _():
        for h in range(n_heads):
            cols = slice(h * LANES, (h + 1) * LANES)
            kr = _rotate_pairs(k_ref[0, :, cols].astype(F32), cosk_ref[0], sink_ref[0], rot_half)
            krot_ref[:, cols] = kr.astype(BF16)

    lam = lam_ref[0, 0]
    tq = q_ref.shape[1]
    first_map = lax.broadcasted_iota(jnp.int32, (tq, LANES), 1) < d

    def attend(q, kp, v_ones):
        s = _dot_nt(q.astype(BF16), kp)
        e = jnp.exp2(s - jnp.max(s, axis=-1, keepdims=True)).astype(BF16)
        o = _dot(e, v_ones)
        return o[:, :LANES] / o[:, LANES:LANES + 1]

    for h in range(n_heads):
        cols = slice(h * LANES, (h + 1) * LANES)
        qr = _rotate_pairs(q_ref[0, :, cols].astype(F32), cosq_ref[0], sinq_ref[0], rot_half)
        qr = qr * (d ** -0.5 * LOG2E)
        kp = krot_ref[:, cols]
        v_ones = jnp.concatenate([v_ref[0, :, cols], ones_ref[...]], axis=1)
        o = attend(jnp.where(first_map, qr, 0.0), kp, v_ones) - lam * attend(jnp.where(first_map, 0.0, qr), kp, v_ones)
        on = _rms(o, gn_ref[:, cols]) * post_scale
        o_ref[0, :, cols] = on.astype(BF16)


def diff_attention(proj, cos_t, sin_t, lam, gn_g, *, rot_dim, lambda_init, d=64, tq=256):
    B, L, N = proj.shape
    W = N // 3
    tq = min(tq, L)
    assert L % tq == 0
    col = lambda c, rows, f: pl.BlockSpec((1, rows, W), f(c))
    q_map = lambda c: (lambda b, i: (b, i, c))
    kv_map = lambda c: (lambda b, i: (b, 0, c))
    ones_col = jnp.broadcast_to((jnp.arange(LANES) == 0).astype(BF16), (L, LANES))
    return pl.pallas_call(
        functools.partial(_diff_attn_kernel, d=d, rot_half=rot_dim // 2, post_scale=1.0 - lambda_init),
        grid=(B, L // tq),
        in_specs=[pl.BlockSpec(memory_space=pltpu.SMEM),
                  col(0, tq, q_map), col(1, L, kv_map), col(2, L, kv_map),
                  pl.BlockSpec((1, L, LANES), lambda b, i: (b, 0, 0)),
                  pl.BlockSpec((1, L, LANES), lambda b, i: (b, 0, 0)),
                  pl.BlockSpec((1, tq, LANES), lambda b, i: (b, i, 0)),
                  pl.BlockSpec((1, tq, LANES), lambda b, i: (b, i, 0)),
                  pl.BlockSpec((1, W), lambda b, i: (0, 0)),
                  pl.BlockSpec((L, LANES), lambda b, i: (0, 0))],
        out_specs=pl.BlockSpec((1, tq, W), lambda b, i: (b, i, 0)),
        out_shape=jax.ShapeDtypeStruct((B, L, W), BF16),
        scratch_shapes=[pltpu.VMEM((L, W), BF16)],
        compiler_params=_params("parallel", "arbitrary"),
        name="diff_attention",
    )(lam.reshape(1, 1).astype(F32), proj, proj, proj, cos_t, sin_t, cos_t, sin_t,
      gn_g.reshape(1, W).astype(F32), ones_col)


def _mix_xattn_kernel(*refs, n_parts, heads):
    parts, w_out_ref = refs[:n_parts], refs[n_parts]
    x_ref, g_ref, wq_ref, kv_ref, wo_ref, o_ref = refs[n_parts + 1:]
    x = x_ref[0]
    D = x.shape[1]
    hd = D // heads
    row = 0
    for a_ref in parts:
        k = a_ref.shape[2]
        x = x + _dot(a_ref[0], w_out_ref[0, row:row + k, :])
        row += k
    hn = _rms(x, g_ref[...]).astype(BF16)
    q = (_dot(hn, wq_ref[0]) * (hd ** -0.5)).astype(BF16)
    outs = []
    for h in range(heads):
        kh = kv_ref[0, :, h * hd:(h + 1) * hd]
        vh = kv_ref[0, :, D + h * hd:D + (h + 1) * hd]
        s = _dot_nt(q[:, h * hd:(h + 1) * hd], kh)
        e = jnp.exp(s - jnp.max(s, axis=-1, keepdims=True))
        oh = _dot(e.astype(BF16), vh) / jnp.sum(e, axis=-1, keepdims=True)
        outs.append(oh.astype(BF16))
    o_ref[0] = x + _dot(jnp.concatenate(outs, axis=-1), wo_ref[0])


def mixer_out_cross_attention(x, parts, w_out, mix_layer, g, wq, kv, wo, layer, *, tq=1024):
    B, L, D = x.shape
    M = kv.shape[1]
    tq = min(tq, L)
    assert L % tq == 0
    tile = lambda width: pl.BlockSpec((1, tq, width), lambda b, i: (b, i, 0))
    return pl.pallas_call(
        functools.partial(_mix_xattn_kernel, n_parts=len(parts), heads=XA_HEADS),
        grid=(B, L // tq),
        in_specs=[tile(a.shape[2]) for a in parts] + [
            _layer_spec(w_out, mix_layer), tile(D), pl.BlockSpec((1, D), lambda b, i: (0, 0)),
            _layer_spec(wq, layer), pl.BlockSpec((1, M, 2 * D), lambda b, i: (b, 0, 0)),
            _layer_spec(wo, layer)],
        out_specs=tile(D),
        out_shape=jax.ShapeDtypeStruct((B, L, D), F32),
        compiler_params=_params("parallel", "parallel"),
        name="mixer_out_cross_attention",
    )(*parts, w_out, x, g.reshape(1, D), wq, kv, wo)


def _swiglu_kernel(x_ref, g_ref, wg_ref, wu_ref, wd_ref, o_ref, *, tf):
    x = x_ref[...]
    hn = _rms(x, g_ref[...]).astype(BF16)
    acc = x
    for c in range(wg_ref.shape[2] // tf):
        cols = slice(c * tf, (c + 1) * tf)
        a = _dot(hn, wg_ref[0, :, cols])
        b = _dot(hn, wu_ref[0, :, cols])
        acc = acc + _dot((a * jax.nn.sigmoid(a) * b).astype(BF16), wd_ref[0, cols, :])
    o_ref[...] = acc


def swiglu_residual(x, g, w_gate, w_up, w_down, layer, *, tm=512, tf=256):
    T, D = x.shape
    FF = w_gate.shape[2]
    tm, tf = min(tm, T), min(tf, FF)
    assert T % tm == 0 and FF % tf == 0
    return pl.pallas_call(
        functools.partial(_swiglu_kernel, tf=tf),
        grid=(T // tm,),
        in_specs=[pl.BlockSpec((tm, D), lambda i: (i, 0)),
                  pl.BlockSpec((1, D), lambda i: (0, 0)),
                  _layer_spec(w_gate, layer), _layer_spec(w_up, layer), _layer_spec(w_down, layer)],
        out_specs=pl.BlockSpec((tm, D), lambda i: (i, 0)),
        out_shape=jax.ShapeDtypeStruct((T, D), F32),
        compiler_params=_params("parallel"),
        name="swiglu_residual",
    )(x, g.reshape(1, D), w_gate, w_up, w_down)


SLOT_TILE = 1024
DISPATCH_TILE = 256
DISPATCH_PIECE = 256
DISPATCH_PIECES = 8
COMBINE_TILE = 512
COMBINE_PIECE = 128
ROUTE_ROWS = 8
_E1, _E2, _W1, _W2, _RANK1, _RANK2 = range(6)
_POS1, _POS2, _GATE1, _GATE2 = range(4)


def _router_kernel(x_ref, g_ref, wr_ref, hn_ref, meta_ref, cum_ref, counts_ref, run_ref, *, n_experts):
    @pl.when(pl.program_id(0) == 0)
    def _():
        run_ref[...] = jnp.zeros_like(run_ref)

    hn = _rms(x_ref[...], g_ref[...])
    hn_ref[...] = hn.astype(BF16)
    logits = jnp.dot(hn, wr_ref[...], preferred_element_type=F32, precision=lax.Precision.HIGHEST)
    tm = logits.shape[0]
    lane = lax.broadcasted_iota(jnp.int32, logits.shape, 1)
    neg = jnp.float32(-jnp.inf)
    logits = jnp.where(lane < n_experts, logits, neg)
    m1 = jnp.max(logits, axis=-1, keepdims=True)
    i1 = jnp.min(jnp.where(logits == m1, lane, LANES), axis=-1, keepdims=True)
    rest = jnp.where(lane == i1, neg, logits)
    m2 = jnp.max(rest, axis=-1, keepdims=True)
    i2 = jnp.min(jnp.where(rest == m2, lane, LANES), axis=-1, keepdims=True)
    e2 = jnp.exp(m2 - m1)
    w1 = 1.0 / (1.0 + e2)
    w2 = e2 / (1.0 + e2)

    chosen = jnp.where((lane == i1) | (lane == i2), 1.0, 0.0)
    r_i = lax.broadcasted_iota(jnp.int32, (tm, tm), 0)
    c_i = lax.broadcasted_iota(jnp.int32, (tm, tm), 1)
    tri = jnp.where(c_i < r_i, 1.0, 0.0).astype(BF16)
    cum = _dot(tri, chosen.astype(BF16)) + run_ref[...]
    cum_ref[...] = cum
    rank1 = jnp.sum(jnp.where(lane == i1, cum, 0.0), axis=-1, keepdims=True)
    rank2 = jnp.sum(jnp.where(lane == i2, cum, 0.0), axis=-1, keepdims=True)
    run_ref[...] += jnp.sum(chosen, axis=0, keepdims=True)
    counts_ref[...] = run_ref[...]

    meta = jnp.zeros_like(logits)
    for idx, val in ((_E1, i1.astype(F32)), (_E2, i2.astype(F32)), (_W1, w1), (_W2, w2),
                     (_RANK1, rank1), (_RANK2, rank2)):
        meta = jnp.where(lane == idx, val, meta)
    meta_ref[...] = meta


def route_tokens(x, g, router_w, *, tm=512):
    T, D = x.shape
    E = router_w.shape[1]
    tm = min(tm, T)
    wr = jnp.pad(router_w.astype(F32), ((0, 0), (0, LANES - E)))
    row = pl.BlockSpec((tm, LANES), lambda i: (i, 0))
    return pl.pallas_call(
        functools.partial(_router_kernel, n_experts=E),
        grid=(T // tm,),
        in_specs=[pl.BlockSpec((tm, D), lambda i: (i, 0)),
                  pl.BlockSpec((1, D), lambda i: (0, 0)),
                  pl.BlockSpec((D, LANES), lambda i: (0, 0))],
        out_specs=[pl.BlockSpec((tm, D), lambda i: (i, 0)), row, row,
                   pl.BlockSpec((1, LANES), lambda i: (0, 0))],
        out_shape=[jax.ShapeDtypeStruct((T, D), BF16), jax.ShapeDtypeStruct((T, LANES), F32),
                   jax.ShapeDtypeStruct((T, LANES), F32), jax.ShapeDtypeStruct((1, LANES), F32)],
        scratch_shapes=[pltpu.VMEM((1, LANES), F32)],
        compiler_params=_params("arbitrary"),
        name="route_tokens",
    )(x, g.reshape(1, D), wr)


def _slots_kernel(off_ref, meta_ref, rec_ref, rows_ref, win_ref, *, n_experts):
    i = pl.program_id(0)
    meta = meta_ref[...]
    e1, e2 = meta[:, _E1:_E1 + 1], meta[:, _E2:_E2 + 1]
    off1, off2 = jnp.zeros_like(e1), jnp.zeros_like(e2)
    for e in range(n_experts):
        off = off_ref[e].astype(F32)
        off1 = jnp.where(e1 == e, off, off1)
        off2 = jnp.where(e2 == e, off, off2)
    lane = lax.broadcasted_iota(jnp.int32, meta.shape, 1)
    rec = jnp.zeros_like(meta)
    for idx, val in ((_POS1, off1 + meta[:, _RANK1:_RANK1 + 1]), (_POS2, off2 + meta[:, _RANK2:_RANK2 + 1]),
                     (_GATE1, meta[:, _W1:_W1 + 1]), (_GATE2, meta[:, _W2:_W2 + 1])):
        rec = jnp.where(lane == idx, val, rec)
    rec_ref[...] = rec
    rows_ref[...] = rec.T[:ROUTE_ROWS, :]

    tm, n_win = meta.shape[0], win_ref.shape[1]
    shift = lambda v, size: lax.shift_right_logical(v, jnp.int32(size.bit_length() - 1))
    tile1 = shift(rec[:, _POS1:_POS1 + 1].astype(jnp.int32), DISPATCH_TILE)
    tile2 = shift(rec[:, _POS2:_POS2 + 1].astype(jnp.int32), DISPATCH_TILE)
    tiles = lax.broadcasted_iota(jnp.int32, (tm, n_win), 1)
    hit = (tile1 == tiles) | (tile2 == tiles)
    block = i * (tm // DISPATCH_PIECE) + shift(lax.broadcasted_iota(jnp.int32, (tm, n_win), 0), DISPATCH_PIECE)
    block = block.astype(F32)
    lo = jnp.min(jnp.where(hit, block, jnp.inf), axis=0, keepdims=True)
    hi = jnp.max(jnp.where(hit, block, -1.0), axis=0, keepdims=True)

    @pl.when(i == 0)
    def _():
        win_ref[...] = jnp.concatenate([jnp.full((1, n_win), jnp.inf, F32), jnp.full((ROUTE_ROWS - 1, n_win), -1.0, F32)],
                                       axis=0)

    win_ref[0:1, :] = jnp.minimum(win_ref[0:1, :], lo)
    win_ref[1:2, :] = jnp.maximum(win_ref[1:2, :], hi)


def assign_slots(meta, group_offset, *, n_experts, n_dispatch_tiles, tm=512):
    T = meta.shape[0]
    tm = min(tm, T)
    assert tm % DISPATCH_PIECE == 0
    n_win = -(-n_dispatch_tiles // LANES) * LANES
    return pl.pallas_call(
        functools.partial(_slots_kernel, n_experts=n_experts),
        grid_spec=pltpu.PrefetchScalarGridSpec(
            num_scalar_prefetch=1,
            grid=(T // tm,),
            in_specs=[pl.BlockSpec((tm, LANES), lambda i, off: (i, 0))],
            out_specs=[pl.BlockSpec((tm, LANES), lambda i, off: (i, 0)),
                       pl.BlockSpec((ROUTE_ROWS, tm), lambda i, off: (0, i)),
                       pl.BlockSpec((ROUTE_ROWS, n_win), lambda i, off: (0, 0))],
        ),
        out_shape=[jax.ShapeDtypeStruct((T, LANES), F32), jax.ShapeDtypeStruct((ROUTE_ROWS, T), F32),
                   jax.ShapeDtypeStruct((ROUTE_ROWS, n_win), F32)],
        compiler_params=_params("arbitrary"),
        name="assign_slots",
    )(group_offset, meta)


def _flatten_items(n_items, n_slots):
    i32 = jnp.int32
    end = jnp.cumsum(n_items).astype(i32)
    start = end - n_items
    total = end[-1]
    w = jnp.arange(n_slots, dtype=i32)
    wc = jnp.minimum(w, total - 1)
    owner = jnp.minimum(jnp.sum(wc[:, None] >= end[None, :], axis=1), n_items.shape[0] - 1).astype(i32)
    return owner, wc - start[owner], w < total, wc, start


def _moe_plan(cum, counts, *, n_experts, expert_base):
    i32 = jnp.int32
    T, E = cum.shape[0], n_experts
    cnt = counts[0, :E].astype(i32)
    tiles_e = (cnt + SLOT_TILE - 1) // SLOT_TILE
    tile_end_e = jnp.cumsum(tiles_e).astype(i32)
    tile_start_e = tile_end_e - tiles_e
    n_used = tile_end_e[-1]
    off_e = tile_start_e * SLOT_TILE

    n_tiles = 2 * T // SLOT_TILE + E
    r = jnp.arange(n_tiles, dtype=i32)
    tile_expert = jnp.minimum(jnp.sum(r[:, None] >= tile_end_e[None, :], axis=1), E - 1).astype(i32)

    n_tok_tiles = T // COMBINE_TILE
    cum_b = cum[::COMBINE_TILE, :E].astype(i32)
    cum_n = jnp.concatenate([cum_b[1:], cnt[None, :]], axis=0)
    s_a, s_b = off_e[None, :] + cum_b, off_e[None, :] + cum_n
    p_lo, p_hi = s_a // COMBINE_PIECE, (s_b - 1) // COMBINE_PIECE
    n_piece = jnp.where(s_b > s_a, p_hi - p_lo + 1, 0).astype(i32)
    piece_end = jnp.cumsum(n_piece, axis=1).astype(i32)
    piece_start = piece_end - n_piece
    n_valid = piece_end[:, -1]
    kk = jnp.minimum(jnp.arange(_combine_pieces(E), dtype=i32)[None, :], n_valid[:, None] - 1)
    owner = jnp.sum(kk[:, :, None] >= piece_end[:, None, :], axis=2).astype(i32)
    is_owner = owner[:, :, None] == jnp.arange(E, dtype=i32)[None, None, :]
    take = lambda a: jnp.sum(jnp.where(is_owner, a[:, None, :], 0), axis=2)
    piece = take(p_lo) + kk - take(piece_start)
    comb = (piece.reshape(-1).astype(i32), n_valid)

    return dict(group_offset=off_e, comb=comb, tile_expert=tile_expert + expert_base,
                n_used=n_used, n_tiles=n_tiles)


def _dispatch_plan(win, n_used, n_tiles, n_tokens, n_experts):
    i32 = jnp.int32
    P = DISPATCH_PIECES
    per = SLOT_TILE // DISPATCH_TILE
    n_disp_tiles = n_tiles * per
    n_blocks = n_tokens // DISPATCH_PIECE
    hi = win[1, :n_disp_tiles].astype(i32)
    lo = jnp.where(hi >= 0, win[0, :n_disp_tiles], 0.0).astype(i32)
    n_p = jnp.where(hi >= 0, hi - lo + 1, 0)
    used = (jnp.arange(n_disp_tiles, dtype=i32) // per) < n_used
    n_it = jnp.where(used, jnp.maximum(1, (n_p + P - 1) // P), 0).astype(i32)
    n_items = (n_experts * n_blocks + n_disp_tiles) // P + 1 + n_disp_tiles
    owner, k, real, wc, start = _flatten_items(n_it, n_items)
    first = lo[owner] + k * P
    n_valid = jnp.where(real, jnp.clip(n_p[owner] - k * P, 0, P), 0).astype(i32)
    flag = jnp.where(real, jnp.where(k == 0, 1, 2), 0).astype(i32)
    blocks = jnp.clip(first[:, None] + jnp.arange(P, dtype=i32)[None, :], 0, n_blocks - 1)
    return owner, flag, n_valid, blocks.reshape(-1).astype(i32)


def _dispatch_kernel(tile_ref, flag_ref, nv_ref, block_ref, *refs, n_pieces):
    rows_refs, hn_refs, o_ref = refs[:n_pieces], refs[n_pieces:2 * n_pieces], refs[2 * n_pieces]
    w = pl.program_id(0)
    flag = flag_ref[w]
    rows, cols = o_ref.shape[0], hn_refs[0].shape[0]
    row = lax.broadcasted_iota(jnp.int32, (rows, cols), 0)

    def one_hot(j):
        base = jnp.where(j < nv_ref[w], tile_ref[w] * rows, -2 * rows)
        slot = (base + row).astype(F32)
        hit = (rows_refs[j][_POS1:_POS1 + 1, :] == slot) | (rows_refs[j][_POS2:_POS2 + 1, :] == slot)
        return jnp.where(hit, 1.0, 0.0).astype(BF16)

    def gathered(j):
        sel = jnp.concatenate([one_hot(j), one_hot(j + 1)], axis=1)
        hn = jnp.concatenate([hn_refs[j][...], hn_refs[j + 1][...]], axis=0)
        return _dot(sel, hn).astype(BF16)

    @pl.when(flag == 1)
    def _():
        o_ref[...] = gathered(0)

    @pl.when(flag == 2)
    def _():
        o_ref[...] += gathered(0)

    for j in range(2, n_pieces, 2):
        @pl.when((flag > 0) & (j < nv_ref[w]))
        def _():
            o_ref[...] += gathered(j)


def dispatch_tokens(hn, rows, items, n_slots):
    T, D = hn.shape
    tile, flag, n_valid, blocks = items
    P = DISPATCH_PIECES
    rows_spec = lambda j: pl.BlockSpec((ROUTE_ROWS, DISPATCH_PIECE), lambda w, t, f, nv, b: (0, b[w * P + j]))
    hn_spec = lambda j: pl.BlockSpec((DISPATCH_PIECE, D), lambda w, t, f, nv, b: (b[w * P + j], 0))
    return pl.pallas_call(
        functools.partial(_dispatch_kernel, n_pieces=P),
        grid_spec=pltpu.PrefetchScalarGridSpec(
            num_scalar_prefetch=4,
            grid=(tile.shape[0],),
            in_specs=[rows_spec(j) for j in range(P)] + [hn_spec(j) for j in range(P)],
            out_specs=pl.BlockSpec((DISPATCH_TILE, D), lambda w, t, f, nv, b: (t[w], 0)),
        ),
        out_shape=jax.ShapeDtypeStruct((n_slots, D), BF16),
        compiler_params=_params("arbitrary"),
        name="moe_dispatch",
    )(tile, flag, n_valid, blocks, *([rows] * P), *([hn] * P))


def _expert_kernel(te_ref, nu_ref, x_ref, wg_ref, wu_ref, wd_ref, o_ref, acc_ref, *, sub):
    r, f = pl.program_id(0), pl.program_id(1)

    @pl.when(f == 0)
    def _():
        acc_ref[...] = jnp.zeros_like(acc_ref)

    @pl.when(r < nu_ref[0])
    def _():
        x = x_ref[...]
        acc = acc_ref[...]
        for c in range(wg_ref.shape[2] // sub):
            cols = slice(c * sub, (c + 1) * sub)
            a = _dot(x, wg_ref[0, :, cols])
            b = _dot(x, wu_ref[0, :, cols])
            acc = acc + _dot((a * jax.nn.sigmoid(a) * b).astype(BF16), wd_ref[0, cols, :])
        acc_ref[...] = acc

    @pl.when(f == pl.num_programs(1) - 1)
    def _():
        o_ref[...] = acc_ref[...].astype(BF16)


def expert_swiglu(xs, plan, w_gate, w_up, w_down, *, tf=1792, sub=256):
    S, D = xs.shape
    FF = w_gate.shape[2]
    tf = tf if FF % tf == 0 else FF
    sub = min(sub, tf)
    assert FF % tf == 0 and tf % sub == 0
    nf = FF // tf
    last = lambda r, nu: jnp.minimum(r, nu[0] - 1)
    col = lambda r, f, nu: jnp.where(r < nu[0], f, nf - 1)
    return pl.pallas_call(
        functools.partial(_expert_kernel, sub=sub),
        grid_spec=pltpu.PrefetchScalarGridSpec(
            num_scalar_prefetch=2,
            grid=(S // SLOT_TILE, nf),
            in_specs=[pl.BlockSpec((SLOT_TILE, D), lambda r, f, te, nu: (last(r, nu), 0)),
                      pl.BlockSpec((1, D, tf), lambda r, f, te, nu: (te[last(r, nu)], 0, col(r, f, nu))),
                      pl.BlockSpec((1, D, tf), lambda r, f, te, nu: (te[last(r, nu)], 0, col(r, f, nu))),
                      pl.BlockSpec((1, tf, D), lambda r, f, te, nu: (te[last(r, nu)], col(r, f, nu), 0))],
            out_specs=pl.BlockSpec((SLOT_TILE, D), lambda r, f, te, nu: (r, 0)),
            scratch_shapes=[pltpu.VMEM((SLOT_TILE, D), F32)],
        ),
        out_shape=jax.ShapeDtypeStruct((S, D), BF16),
        compiler_params=_params("arbitrary", "arbitrary"),
        name="moe_experts",
    )(plan["tile_expert"], plan["n_used"].reshape(1), xs, w_gate, w_up, w_down)


def _combine_pieces(n_experts):
    return 2 * COMBINE_TILE // COMBINE_PIECE + 2 * n_experts


def _combine_kernel(piece_ref, nv_ref, rec_ref, x_ref, g_ref, *rest, n_pieces, normalize):
    y_refs, o_ref = rest[:n_pieces], rest[n_pieces]
    i = pl.program_id(0)
    rows = o_ref.shape[0]
    rec = rec_ref[...]
    pos1, pos2 = rec[:, _POS1:_POS1 + 1], rec[:, _POS2:_POS2 + 1]
    gate1, gate2 = rec[:, _GATE1:_GATE1 + 1], rec[:, _GATE2:_GATE2 + 1]
    lane = lax.broadcasted_iota(jnp.int32, (rows, COMBINE_PIECE), 1)

    def weights(k):
        base = jnp.where(k < nv_ref[i], piece_ref[i * n_pieces + k] * COMBINE_PIECE, -COMBINE_PIECE)
        slot = (base + lane).astype(F32)
        return (jnp.where(pos1 == slot, gate1, 0.0) + jnp.where(pos2 == slot, gate2, 0.0)).astype(BF16)

    o_ref[...] = x_ref[...]
    for k in range(0, n_pieces, 2):
        @pl.when(k < nv_ref[i])
        def _():
            sel = jnp.concatenate([weights(k), weights(k + 1)], axis=1)
            y = jnp.concatenate([y_refs[k][...], y_refs[k + 1][...]], axis=0)
            o_ref[...] += _dot(sel, y)

    if normalize:
        o_ref[...] = _rms(o_ref[...], g_ref[...])


def combine_tokens(x, y, rec, plan, *, n_experts, final_g=None):
    T, D = x.shape
    piece, n_valid = plan["comb"]
    n_pieces = _combine_pieces(n_experts)
    g = jnp.ones((1, D), F32) if final_g is None else final_g.reshape(1, D).astype(F32)
    y_spec = lambda k: pl.BlockSpec((COMBINE_PIECE, D), lambda i, p, nv: (p[i * n_pieces + k], 0))
    return pl.pallas_call(
        functools.partial(_combine_kernel, n_pieces=n_pieces, normalize=final_g is not None),
        grid_spec=pltpu.PrefetchScalarGridSpec(
            num_scalar_prefetch=2,
            grid=(T // COMBINE_TILE,),
            in_specs=[pl.BlockSpec((COMBINE_TILE, LANES), lambda i, p, nv: (i, 0)),
                      pl.BlockSpec((COMBINE_TILE, D), lambda i, p, nv: (i, 0)),
                      pl.BlockSpec((1, D), lambda i, p, nv: (0, 0))] + [y_spec(k) for k in range(n_pieces)],
            out_specs=pl.BlockSpec((COMBINE_TILE, D), lambda i, p, nv: (i, 0)),
        ),
        out_shape=jax.ShapeDtypeStruct((T, D), F32),
        compiler_params=_params("parallel"),
        name="moe_combine",
    )(piece, n_valid, rec, x, g, *([y] * n_pieces))


def moe_residual(x, g, router_w, w_gate, w_up, w_down, layer, *, final_g=None):
    E = router_w.shape[1]
    hn, meta, cum, counts = route_tokens(x, g, router_w)
    plan = _moe_plan(cum, counts, n_experts=E, expert_base=layer * E)
    n_tiles = plan["n_tiles"]
    rec, rows, win = assign_slots(meta, plan["group_offset"], n_experts=E,
                                  n_dispatch_tiles=n_tiles * (SLOT_TILE // DISPATCH_TILE))
    items = _dispatch_plan(win, plan["n_used"], n_tiles, x.shape[0], E)
    xs = dispatch_tokens(hn, rows, items, n_tiles * SLOT_TILE)
    y = expert_swiglu(xs, plan, w_gate, w_up, w_down)
    return combine_tokens(x, y, rec, plan, n_experts=E, final_g=final_g)


def _final_norm_kernel(x_ref, g_ref, o_ref):
    o_ref[...] = _rms(x_ref[...], g_ref[...])


def final_norm(x, g, *, tm=1024):
    T, D = x.shape
    tm = min(tm, T)
    return pl.pallas_call(
        _final_norm_kernel,
        grid=(T // tm,),
        in_specs=[pl.BlockSpec((tm, D), lambda i: (i, 0)), pl.BlockSpec((1, D), lambda i: (0, 0))],
        out_specs=pl.BlockSpec((tm, D), lambda i: (i, 0)),
        out_shape=jax.ShapeDtypeStruct((T, D), F32),
        compiler_params=_params("parallel"),
        name="final_norm",
    )(x, g.reshape(1, D))


def kernel(x, mem, positions, mem_norm_g, final_norm_g, mix_norm_g, xa_norm_g, ffn_norm_g, xa_wq, xa_wkv, xa_wo, ev_w_in, ev_conv_w, ev_ret_decay_f, ev_ret_decay_b, ev_ret_gn_g, ev_w_out, ffd_w_gate, ffd_w_up, ffd_w_down, od_w_in, od_lam_q1, od_lam_k1, od_lam_q2, od_lam_k2, od_gn_g, od_w_out, moe_router, moe_w_gate, moe_w_up, moe_w_down):
    B, L, D = x.shape
    M = mem.shape[1]
    T = B * L
    depth = mix_norm_g.shape[0]
    conv_ch = ev_conv_w.shape[2]
    ret_w = ev_ret_gn_g.shape[1]
    ret_dk = ret_w // RET_HEADS
    diff_d = od_lam_q1.shape[1]
    bf = lambda w: w.astype(BF16)
    experts = lambda w: bf(w).reshape((-1,) + w.shape[2:])

    xa_wq, xa_wkv, xa_wo = bf(xa_wq), bf(xa_wkv), bf(xa_wo)
    ev_w_in, ev_w_out, od_w_in, od_w_out = bf(ev_w_in), bf(ev_w_out), bf(od_w_in), bf(od_w_out)
    ffd_w_gate, ffd_w_up, ffd_w_down = bf(ffd_w_gate), bf(ffd_w_up), bf(ffd_w_down)
    moe_w_gate, moe_w_up, moe_w_down = experts(moe_w_gate), experts(moe_w_up), experts(moe_w_down)
    ev_conv_w = ev_conv_w.astype(F32)

    ret_cos, ret_sin = _rope_tables(positions, ret_dk, RET_THETA)
    diff_cos, diff_sin = _rope_tables(positions, diff_d // 4, ROPE_THETA)
    mem2 = mem.reshape(B * M, D)
    xt = x.reshape(T, D)

    for i in range(depth):
        j = i // 2
        if i % 2 == 0:
            proj = norm_proj(xt, mix_norm_g[i], ev_w_in, j).reshape(B, L, -1)
            a = short_conv(proj, ev_conv_w, j, ch=conv_ch)
            r = retention(proj, ret_cos, ret_sin, ev_ret_decay_f[j], ev_ret_decay_b[j], ev_ret_gn_g[j],
                          col0=3 * conv_ch // ret_w, width=ret_w, dk=ret_dk)
            parts, w_out = [a, r], ev_w_out
        else:
            lambda_init = 0.8 - 0.6 * math.exp(-0.3 * i)
            f32 = lambda v: v.astype(F32)
            lam = (jnp.exp(jnp.sum(f32(od_lam_q1[j]) * f32(od_lam_k1[j])))
                   - jnp.exp(jnp.sum(f32(od_lam_q2[j]) * f32(od_lam_k2[j]))) + lambda_init)
            proj = norm_proj(xt, mix_norm_g[i], od_w_in, j).reshape(B, L, -1)
            o = diff_attention(proj, diff_cos, diff_sin, lam, od_gn_g[j], rot_dim=diff_d // 4,
                               lambda_init=lambda_init, d=diff_d)
            parts, w_out = [o], od_w_out

        kv = norm_proj(mem2, mem_norm_g, xa_wkv, i).reshape(B, M, 2 * D)
        xt = mixer_out_cross_attention(xt.reshape(B, L, D), parts, w_out, j, xa_norm_g[i], xa_wq, kv, xa_wo,
                                       i).reshape(T, D)

        if i % 2 == 0:
            xt = swiglu_residual(xt, ffn_norm_g[i], ffd_w_gate, ffd_w_up, ffd_w_down, j)
        else:
            xt = moe_residual(xt, ffn_norm_g[i], moe_router[j], moe_w_gate, moe_w_up, moe_w_down, j,
                              final_g=final_norm_g if i == depth - 1 else None)

    if depth % 2 == 1:
        xt = final_norm(xt, final_norm_g)
    return xt.reshape(B, L, D)
```

```python
import functools
import math

import jax
import jax.numpy as jnp
from jax import lax
from jax.experimental import pallas as pl
from jax.experimental.pallas import tpu as pltpu

F32 = jnp.float32
BF16 = jnp.bfloat16
EPS = 1e-6
LANES = 128
LOG2E = 1.4426950408889634

RET_HEADS = 8
RET_THETA = 10000.0
RET_CHUNK = 256
ROPE_THETA = 500000.0
XA_HEADS = 4

VMEM_LIMIT = 56 * 1024 * 1024


def _params(*sem, vmem_limit=VMEM_LIMIT):
    return pltpu.CompilerParams(dimension_semantics=sem, vmem_limit_bytes=vmem_limit)


def _rms(x, g):
    ms = jnp.mean(x * x, axis=-1, keepdims=True)
    return x * lax.rsqrt(ms + EPS) * g


def _dot(a, b):
    return jnp.dot(a, b, preferred_element_type=F32)


def _dot_nt(a, b):
    return lax.dot_general(a, b, (((1,), (1,)), ((), ())), preferred_element_type=F32)


def _rotate_pairs(x, cos_t, sin_t, half):
    lane = lax.broadcasted_iota(jnp.int32, x.shape, 1) % 64
    partner = jnp.where(lane < half, pltpu.roll(x, LANES - half, axis=1), pltpu.roll(x, half, axis=1))
    return x * cos_t + partner * sin_t


def _layer_spec(w, layer):
    return pl.BlockSpec((1,) + w.shape[1:], lambda *_: (layer, 0, 0))


def _norm_proj_kernel(x_ref, g_ref, w_ref, o_ref, *, tn):
    hn = _rms(x_ref[...], g_ref[...]).astype(BF16)
    for c in range(o_ref.shape[1] // tn):
        cols = slice(c * tn, (c + 1) * tn)
        o_ref[:, cols] = _dot(hn, w_ref[0, :, cols]).astype(o_ref.dtype)


def norm_proj(x, g, w, layer, *, tm=1024, tn=512):
    T, D = x.shape
    N = w.shape[2]
    tm, tn = min(tm, T), min(tn, N)
    assert T % tm == 0 and N % tn == 0
    return pl.pallas_call(
        functools.partial(_norm_proj_kernel, tn=tn),
        grid=(T // tm,),
        in_specs=[pl.BlockSpec((tm, D), lambda i: (i, 0)),
                  pl.BlockSpec((1, D), lambda i: (0, 0)),
                  _layer_spec(w, layer)],
        out_specs=pl.BlockSpec((tm, N), lambda i: (i, 0)),
        out_shape=jax.ShapeDtypeStruct((T, N), BF16),
        compiler_params=_params("parallel"),
        name="norm_proj",
    )(x, g.reshape(1, D), w)


def _conv_kernel(gb_ref, gc_ref, h_ref, w_ref, o_ref):
    u = gc_ref[0].astype(F32) * h_ref[0].astype(F32)
    L = u.shape[0]
    row = lax.broadcasted_iota(jnp.int32, u.shape, 0)
    prev = jnp.where(row == 0, 0.0, pltpu.roll(u, 1, axis=0))
    nxt = jnp.where(row == L - 1, 0.0, pltpu.roll(u, L - 1, axis=0))
    conv = w_ref[0, 0:1, :] * prev + w_ref[0, 1:2, :] * u + w_ref[0, 2:3, :] * nxt
    o_ref[0] = (gb_ref[0].astype(F32) * conv).astype(BF16)


def short_conv(proj, conv_w, layer, *, ch):
    B, L, _ = proj.shape
    col = lambda c: pl.BlockSpec((1, L, ch), lambda b: (b, 0, c))
    return pl.pallas_call(
        _conv_kernel,
        grid=(B,),
        in_specs=[col(0), col(1), col(2), _layer_spec(conv_w, layer)],
        out_specs=pl.BlockSpec((1, L, ch), lambda b: (b, 0, 0)),
        out_shape=jax.ShapeDtypeStruct((B, L, ch), BF16),
        compiler_params=_params("parallel"),
        name="short_conv",
    )(proj, proj, proj, conv_w)


def _retention_kernel(q_ref, k_ref, v_ref, g_ref, cos_ref, sin_ref, dmat_ref, qdec_ref, kdecf_ref,
                      kdecb_ref, cdecf_ref, cdecb_ref, gn_ref, o_ref, qrot_ref, krot_ref, bst_ref,
                      fstate_ref, bstate_ref, *, chunk, dk):
    L = q_ref.shape[1]
    n_pairs = q_ref.shape[2] // LANES
    nc = L // chunk
    C = chunk
    cos_t, sin_t = cos_ref[0], sin_ref[0]

    for p in range(n_pairs):
        cols = slice(p * LANES, (p + 1) * LANES)
        qrot_ref[:, cols] = _rotate_pairs(q_ref[0, :, cols].astype(F32), cos_t, sin_t, dk // 2).astype(BF16)
        kr = _rotate_pairs(k_ref[0, :, cols].astype(F32), cos_t, sin_t, dk // 2)
        krot_ref[:, cols] = (kr * (dk ** -0.5)).astype(BF16)

    lane_c = lax.broadcasted_iota(jnp.int32, (C, LANES), 1)
    first_head = lane_c < dk
    r_i = lax.broadcasted_iota(jnp.int32, (LANES, LANES), 0) // dk
    c_i = lax.broadcasted_iota(jnp.int32, (LANES, LANES), 1) // dk
    same_head = r_i == c_i

    def kv_outer(kp, kdec, vp):
        kd_t = (kp.astype(F32) * kdec).T.astype(BF16)
        return jnp.where(same_head, _dot(kd_t, vp), 0.0)

    bstate_ref[...] = jnp.zeros_like(bstate_ref)

    def back_body(t, carry):
        i = nc - 1 - t
        rows = pl.ds(pl.multiple_of(i * C, C), C)
        for p in range(n_pairs):
            cols = slice(p * LANES, (p + 1) * LANES)
            bst_ref[i, p] = bstate_ref[p]
            upd = kv_outer(krot_ref[rows, cols], kdecb_ref[p], v_ref[0, rows, cols])
            bstate_ref[p] = cdecb_ref[p] * bstate_ref[p] + upd
        return carry

    lax.fori_loop(0, nc, back_body, 0)

    fstate_ref[...] = jnp.zeros_like(fstate_ref)

    def fwd_body(i, carry):
        rows = pl.ds(pl.multiple_of(i * C, C), C)
        for p in range(n_pairs):
            cols = slice(p * LANES, (p + 1) * LANES)
            qp, kp, vp = qrot_ref[rows, cols], krot_ref[rows, cols], v_ref[0, rows, cols]
            zero = jnp.zeros_like(qp)
            s0 = _dot_nt(jnp.where(first_head, qp, zero), kp) * dmat_ref[2 * p]
            s1 = _dot_nt(jnp.where(first_head, zero, qp), kp) * dmat_ref[2 * p + 1]
            intra = jnp.where(first_head, _dot(s0.astype(BF16), vp), _dot(s1.astype(BF16), vp))
            q32 = qp.astype(F32)
            qd = qdec_ref[p]
            qq = jnp.concatenate([q32 * qd[:, :LANES], q32 * qd[:, LANES:]], axis=1).astype(BF16)
            st = jnp.concatenate([fstate_ref[p], bst_ref[i, p]], axis=0).astype(BF16)
            o = intra + _dot(qq, st)
            o2 = o * o
            s_first = jnp.sum(jnp.where(first_head, o2, 0.0), axis=-1, keepdims=True)
            s_all = jnp.sum(o2, axis=-1, keepdims=True)
            ms = jnp.where(first_head, s_first, s_all - s_first) * (1.0 / dk)
            on = o * lax.rsqrt(ms + EPS) * gn_ref[:, cols]
            g = g_ref[0, rows, cols].astype(F32)
            o_ref[0, rows, cols] = (g * jax.nn.sigmoid(g) * on).astype(BF16)
            upd = kv_outer(kp, kdecf_ref[p], vp)
            fstate_ref[p] = cdecf_ref[p] * fstate_ref[p] + upd
        return carry

    lax.fori_loop(0, nc, fwd_body, 0)


def _retention_tables(p_fwd, p_bwd, C, dk):
    H = p_fwd.shape[0]
    lg_f = -jnp.exp(p_fwd.astype(F32))
    lg_b = -jnp.exp(p_bwd.astype(F32))
    idx = jnp.arange(C, dtype=F32)
    dist = idx[:, None] - idx[None, :]
    dmat = jnp.where(dist >= 0,
                     jnp.exp(lg_f[:, None, None] * jnp.maximum(dist, 0.0)),
                     jnp.exp(lg_b[:, None, None] * jnp.maximum(-dist, 0.0)))

    def lanes(per_head):
        t = jnp.repeat(per_head[:, :, None], dk, axis=2)
        return t.reshape(H // 2, 2, C, dk).transpose(0, 2, 1, 3).reshape(H // 2, C, 2 * dk)

    qdec_f = lanes(jnp.exp(lg_f[:, None] * (idx + 1.0)))
    qdec_b = lanes(jnp.exp(lg_b[:, None] * (C - idx)))
    kdec_f = lanes(jnp.exp(lg_f[:, None] * (C - 1.0 - idx)))
    kdec_b = lanes(jnp.exp(lg_b[:, None] * idx))
    qdec = jnp.concatenate([qdec_f, qdec_b], axis=2)

    def rows(lg):
        per_row = jnp.repeat(jnp.exp(lg * C).reshape(H // 2, 2), dk, axis=1)
        return jnp.broadcast_to(per_row[:, :, None], (H // 2, 2 * dk, 2 * dk))

    return dmat, qdec, kdec_f, kdec_b, rows(lg_f), rows(lg_b)


def _rope_tables(positions, rot_dim, theta, head_dim=64):
    half = rot_dim // 2
    j = jnp.arange(LANES) % head_dim
    inv_freq = jnp.exp(-math.log(theta) * (j % half).astype(F32) * (2.0 / rot_dim))
    ang = positions.astype(F32)[:, :, None] * inv_freq
    cos_t = jnp.where(j < rot_dim, jnp.cos(ang), 1.0)
    sin_t = jnp.where(j < rot_dim, jnp.where(j < half, -jnp.sin(ang), jnp.sin(ang)), 0.0)
    return cos_t, sin_t


def retention(proj, cos_t, sin_t, p_fwd, p_bwd, gn_g, *, col0, width, dk=64):
    B, L, _ = proj.shape
    C = min(RET_CHUNK, L)
    assert L % C == 0 and width % LANES == 0
    n_pairs = width // LANES
    nc = L // C
    tables = _retention_tables(p_fwd, p_bwd, C, dk)
    col = lambda c: pl.BlockSpec((1, L, width), lambda b: (b, 0, col0 + c))
    rope = pl.BlockSpec((1, L, LANES), lambda b: (b, 0, 0))
    full = lambda a: pl.BlockSpec(a.shape, lambda b: (0,) * a.ndim)
    gn = gn_g.reshape(1, width).astype(F32)
    return pl.pallas_call(
        functools.partial(_retention_kernel, chunk=C, dk=dk),
        grid=(B,),
        in_specs=[col(0), col(1), col(2), col(3), rope, rope] + [full(t) for t in tables] + [full(gn)],
        out_specs=pl.BlockSpec((1, L, width), lambda b: (b, 0, 0)),
        out_shape=jax.ShapeDtypeStruct((B, L, width), BF16),
        scratch_shapes=[pltpu.VMEM((L, width), BF16), pltpu.VMEM((L, width), BF16),
                        pltpu.VMEM((nc, n_pairs, LANES, LANES), F32),
                        pltpu.VMEM((n_pairs, LANES, LANES), F32),
                        pltpu.VMEM((n_pairs, LANES, LANES), F32)],
        compiler_params=_params("parallel"),
        name="retention",
    )(proj, proj, proj, proj, cos_t, sin_t, *tables, gn)


def _diff_attn_kernel(lam_ref, q_ref, k_ref, v_ref, cosk_ref, sink_ref, cosq_ref, sinq_ref, gn_ref, ones_ref,
                      o_ref, krot_ref, *, d, rot_half, post_scale):
    n_heads = q_ref.shape[2] // LANES

    @pl.when(pl.program_id(1) == 0)
    def _():
        for h in range(n_heads):
            cols = slice(h * LANES, (h + 1) * LANES)
            kr = _rotate_pairs(k_ref[0, :, cols].astype(F32), cosk_ref[0], sink_ref[0], rot_half)
            krot_ref[:, cols] = kr.astype(BF16)

    lam = lam_ref[0, 0]
    tq = q_ref.shape[1]
    first_map = lax.broadcasted_iota(jnp.int32, (tq, LANES), 1) < d

    def attend(q, kp, v_ones):
        s = _dot_nt(q.astype(BF16), kp)
        e = jnp.exp2(s - jnp.max(s, axis=-1, keepdims=True)).astype(BF16)
        o = _dot(e, v_ones)
        return o[:, :LANES] / o[:, LANES:LANES + 1]

    for h in range(n_heads):
        cols = slice(h * LANES, (h + 1) * LANES)
        qr = _rotate_pairs(q_ref[0, :, cols].astype(F32), cosq_ref[0], sinq_ref[0], rot_half)
        qr = qr * (d ** -0.5 * LOG2E)
        kp = krot_ref[:, cols]
        v_ones = jnp.concatenate([v_ref[0, :, cols], ones_ref[...]], axis=1)
        o = attend(jnp.where(first_map, qr, 0.0), kp, v_ones) - lam * attend(jnp.where(first_map, 0.0, qr), kp, v_ones)
        on = _rms(o, gn_ref[:, cols]) * post_scale
        o_ref[0, :, cols] = on.astype(BF16)


def diff_attention(proj, cos_t, sin_t, lam, gn_g, *, rot_dim, lambda_init, d=64, tq=256):
    B, L, N = proj.shape
    W = N // 3
    tq = min(tq, L)
    assert L % tq == 0
    col = lambda c, rows, f: pl.BlockSpec((1, rows, W), f(c))
    q_map = lambda c: (lambda b, i: (b, i, c))
    kv_map = lambda c: (lambda b, i: (b, 0, c))
    ones_col = jnp.broadcast_to((jnp.arange(LANES) == 0).astype(BF16), (L, LANES))
    return pl.pallas_call(
        functools.partial(_diff_attn_kernel, d=d, rot_half=rot_dim // 2, post_scale=1.0 - lambda_init),
        grid=(B, L // tq),
        in_specs=[pl.BlockSpec(memory_space=pltpu.SMEM),
                  col(0, tq, q_map), col(1, L, kv_map), col(2, L, kv_map),
                  pl.BlockSpec((1, L, LANES), lambda b, i: (b, 0, 0)),
                  pl.BlockSpec((1, L, LANES), lambda b, i: (b, 0, 0)),
                  pl.BlockSpec((1, tq, LANES), lambda b, i: (b, i, 0)),
                  pl.BlockSpec((1, tq, LANES), lambda b, i: (b, i, 0)),
                  pl.BlockSpec((1, W), lambda b, i: (0, 0)),
                  pl.BlockSpec((L, LANES), lambda b, i: (0, 0))],
        out_specs=pl.BlockSpec((1, tq, W), lambda b, i: (b, i, 0)),
        out_shape=jax.ShapeDtypeStruct((B, L, W), BF16),
        scratch_shapes=[pltpu.VMEM((L, W), BF16)],
        compiler_params=_params("parallel", "arbitrary"),
        name="diff_attention",
    )(lam.reshape(1, 1).astype(F32), proj, proj, proj, cos_t, sin_t, cos_t, sin_t,
      gn_g.reshape(1, W).astype(F32), ones_col)


def _mix_xattn_kernel(*refs, n_parts, heads):
    parts, w_out_ref = refs[:n_parts], refs[n_parts]
    x_ref, g_ref, wq_ref, kv_ref, wo_ref, o_ref = refs[n_parts + 1:]
    x = x_ref[0]
    D = x.shape[1]
    hd = D // heads
    row = 0
    for a_ref in parts:
        k = a_ref.shape[2]
        x = x + _dot(a_ref[0], w_out_ref[0, row:row + k, :])
        row += k
    hn = _rms(x, g_ref[...]).astype(BF16)
    q = (_dot(hn, wq_ref[0]) * (hd ** -0.5)).astype(BF16)
    outs = []
    for h in range(heads):
        kh = kv_ref[0, :, h * hd:(h + 1) * hd]
        vh = kv_ref[0, :, D + h * hd:D + (h + 1) * hd]
        s = _dot_nt(q[:, h * hd:(h + 1) * hd], kh)
        e = jnp.exp(s - jnp.max(s, axis=-1, keepdims=True))
        oh = _dot(e.astype(BF16), vh) / jnp.sum(e, axis=-1, keepdims=True)
        outs.append(oh.astype(BF16))
    o_ref[0] = x + _dot(jnp.concatenate(outs, axis=-1), wo_ref[0])


def mixer_out_cross_attention(x, parts, w_out, mix_layer, g, wq, kv, wo, layer, *, tq=1024):
    B, L, D = x.shape
    M = kv.shape[1]
    tq = min(tq, L)
    assert L % tq == 0
    tile = lambda width: pl.BlockSpec((1, tq, width), lambda b, i: (b, i, 0))
    return pl.pallas_call(
        functools.partial(_mix_xattn_kernel, n_parts=len(parts), heads=XA_HEADS),
        grid=(B, L // tq),
        in_specs=[tile(a.shape[2]) for a in parts] + [
            _layer_spec(w_out, mix_layer), tile(D), pl.BlockSpec((1, D), lambda b, i: (0, 0)),
            _layer_spec(wq, layer), pl.BlockSpec((1, M, 2 * D), lambda b, i: (b, 0, 0)),
            _layer_spec(wo, layer)],
        out_specs=tile(D),
        out_shape=jax.ShapeDtypeStruct((B, L, D), F32),
        compiler_params=_params("parallel", "parallel"),
        name="mixer_out_cross_attention",
    )(*parts, w_out, x, g.reshape(1, D), wq, kv, wo)


def _swiglu_kernel(x_ref, g_ref, wg_ref, wu_ref, wd_ref, o_ref, *, tf):
    x = x_ref[...]
    hn = _rms(x, g_ref[...]).astype(BF16)
    acc = x
    for c in range(wg_ref.shape[2] // tf):
        cols = slice(c * tf, (c + 1) * tf)
        a = _dot(hn, wg_ref[0, :, cols])
        b = _dot(hn, wu_ref[0, :, cols])
        acc = acc + _dot((a * jax.nn.sigmoid(a) * b).astype(BF16), wd_ref[0, cols, :])
    o_ref[...] = acc


def swiglu_residual(x, g, w_gate, w_up, w_down, layer, *, tm=512, tf=256):
    T, D = x.shape
    FF = w_gate.shape[2]
    tm, tf = min(tm, T), min(tf, FF)
    assert T % tm == 0 and FF % tf == 0
    return pl.pallas_call(
        functools.partial(_swiglu_kernel, tf=tf),
        grid=(T // tm,),
        in_specs=[pl.BlockSpec((tm, D), lambda i: (i, 0)),
                  pl.BlockSpec((1, D), lambda i: (0, 0)),
                  _layer_spec(w_gate, layer), _layer_spec(w_up, layer), _layer_spec(w_down, layer)],
        out_specs=pl.BlockSpec((tm, D), lambda i: (i, 0)),
        out_shape=jax.ShapeDtypeStruct((T, D), F32),
        compiler_params=_params("parallel"),
        name="swiglu_residual",
    )(x, g.reshape(1, D), w_gate, w_up, w_down)


SLOT_TILE = 1024
DISPATCH_TILE = 256
DISPATCH_PIECE = 256
DISPATCH_PIECES = 8
COMBINE_TILE = 512
COMBINE_PIECE = 128
ROUTE_ROWS = 8
_E1, _E2, _W1, _W2, _RANK1, _RANK2 = range(6)
_POS1, _POS2, _GATE1, _GATE2 = range(4)


def _router_kernel(x_ref, g_ref, wr_ref, hn_ref, meta_ref, cum_ref, counts_ref, run_ref, *, n_experts):
    @pl.when(pl.program_id(0) == 0)
    def _():
        run_ref[...] = jnp.zeros_like(run_ref)

    hn = _rms(x_ref[...], g_ref[...])
    hn_ref[...] = hn.astype(BF16)
    logits = jnp.dot(hn, wr_ref[...], preferred_element_type=F32, precision=lax.Precision.HIGHEST)
    tm = logits.shape[0]
    lane = lax.broadcasted_iota(jnp.int32, logits.shape, 1)
    neg = jnp.float32(-jnp.inf)
    logits = jnp.where(lane < n_experts, logits, neg)
    m1 = jnp.max(logits, axis=-1, keepdims=True)
    i1 = jnp.min(jnp.where(logits == m1, lane, LANES), axis=-1, keepdims=True)
    rest = jnp.where(lane == i1, neg, logits)
    m2 = jnp.max(rest, axis=-1, keepdims=True)
    i2 = jnp.min(jnp.where(rest == m2, lane, LANES), axis=-1, keepdims=True)
    e2 = jnp.exp(m2 - m1)
    w1 = 1.0 / (1.0 + e2)
    w2 = e2 / (1.0 + e2)

    chosen = jnp.where((lane == i1) | (lane == i2), 1.0, 0.0)
    r_i = lax.broadcasted_iota(jnp.int32, (tm, tm), 0)
    c_i = lax.broadcasted_iota(jnp.int32, (tm, tm), 1)
    tri = jnp.where(c_i < r_i, 1.0, 0.0).astype(BF16)
    cum = _dot(tri, chosen.astype(BF16)) + run_ref[...]
    cum_ref[...] = cum
    rank1 = jnp.sum(jnp.where(lane == i1, cum, 0.0), axis=-1, keepdims=True)
    rank2 = jnp.sum(jnp.where(lane == i2, cum, 0.0), axis=-1, keepdims=True)
    run_ref[...] += jnp.sum(chosen, axis=0, keepdims=True)
    counts_ref[...] = run_ref[...]

    meta = jnp.zeros_like(logits)
    for idx, val in ((_E1, i1.astype(F32)), (_E2, i2.astype(F32)), (_W1, w1), (_W2, w2),
                     (_RANK1, rank1), (_RANK2, rank2)):
        meta = jnp.where(lane == idx, val, meta)
    meta_ref[...] = meta


def route_tokens(x, g, router_w, *, tm=512):
    T, D = x.shape
    E = router_w.shape[1]
    tm = min(tm, T)
    wr = jnp.pad(router_w.astype(F32), ((0, 0), (0, LANES - E)))
    row = pl.BlockSpec((tm, LANES), lambda i: (i, 0))
    return pl.pallas_call(
        functools.partial(_router_kernel, n_experts=E),
        grid=(T // tm,),
        in_specs=[pl.BlockSpec((tm, D), lambda i: (i, 0)),
                  pl.BlockSpec((1, D), lambda i: (0, 0)),
                  pl.BlockSpec((D, LANES), lambda i: (0, 0))],
        out_specs=[pl.BlockSpec((tm, D), lambda i: (i, 0)), row, row,
                   pl.BlockSpec((1, LANES), lambda i: (0, 0))],
        out_shape=[jax.ShapeDtypeStruct((T, D), BF16), jax.ShapeDtypeStruct((T, LANES), F32),
                   jax.ShapeDtypeStruct((T, LANES), F32), jax.ShapeDtypeStruct((1, LANES), F32)],
        scratch_shapes=[pltpu.VMEM((1, LANES), F32)],
        compiler_params=_params("arbitrary"),
        name="route_tokens",
    )(x, g.reshape(1, D), wr)


def _slots_kernel(off_ref, meta_ref, rec_ref, rows_ref, win_ref, *, n_experts):
    i = pl.program_id(0)
    meta = meta_ref[...]
    e1, e2 = meta[:, _E1:_E1 + 1], meta[:, _E2:_E2 + 1]
    off1, off2 = jnp.zeros_like(e1), jnp.zeros_like(e2)
    for e in range(n_experts):
        off = off_ref[e].astype(F32)
        off1 = jnp.where(e1 == e, off, off1)
        off2 = jnp.where(e2 == e, off, off2)
    lane = lax.broadcasted_iota(jnp.int32, meta.shape, 1)
    rec = jnp.zeros_like(meta)
    for idx, val in ((_POS1, off1 + meta[:, _RANK1:_RANK1 + 1]), (_POS2, off2 + meta[:, _RANK2:_RANK2 + 1]),
                     (_GATE1, meta[:, _W1:_W1 + 1]), (_GATE2, meta[:, _W2:_W2 + 1])):
        rec = jnp.where(lane == idx, val, rec)
    rec_ref[...] = rec
    rows_ref[...] = rec.T[:ROUTE_ROWS, :]

    tm, n_win = meta.shape[0], win_ref.shape[1]
    shift = lambda v, size: lax.shift_right_logical(v, jnp.int32(size.bit_length() - 1))
    tile1 = shift(rec[:, _POS1:_POS1 + 1].astype(jnp.int32), DISPATCH_TILE)
    tile2 = shift(rec[:, _POS2:_POS2 + 1].astype(jnp.int32), DISPATCH_TILE)
    tiles = lax.broadcasted_iota(jnp.int32, (tm, n_win), 1)
    hit = (tile1 == tiles) | (tile2 == tiles)
    block = i * (tm // DISPATCH_PIECE) + shift(lax.broadcasted_iota(jnp.int32, (tm, n_win), 0), DISPATCH_PIECE)
    block = block.astype(F32)
    lo = jnp.min(jnp.where(hit, block, jnp.inf), axis=0, keepdims=True)
    hi = jnp.max(jnp.where(hit, block, -1.0), axis=0, keepdims=True)

    @pl.when(i == 0)
    def _():
        win_ref[...] = jnp.concatenate([jnp.full((1, n_win), jnp.inf, F32), jnp.full((ROUTE_ROWS - 1, n_win), -1.0, F32)],
                                       axis=0)

    win_ref[0:1, :] = jnp.minimum(win_ref[0:1, :], lo)
    win_ref[1:2, :] = jnp.maximum(win_ref[1:2, :], hi)


def assign_slots(meta, group_offset, *, n_experts, n_dispatch_tiles, tm=512):
    T = meta.shape[0]
    tm = min(tm, T)
    assert tm % DISPATCH_PIECE == 0
    n_win = -(-n_dispatch_tiles // LANES) * LANES
    return pl.pallas_call(
        functools.partial(_slots_kernel, n_experts=n_experts),
        grid_spec=pltpu.PrefetchScalarGridSpec(
            num_scalar_prefetch=1,
            grid=(T // tm,),
            in_specs=[pl.BlockSpec((tm, LANES), lambda i, off: (i, 0))],
            out_specs=[pl.BlockSpec((tm, LANES), lambda i, off: (i, 0)),
                       pl.BlockSpec((ROUTE_ROWS, tm), lambda i, off: (0, i)),
                       pl.BlockSpec((ROUTE_ROWS, n_win), lambda i, off: (0, 0))],
        ),
        out_shape=[jax.ShapeDtypeStruct((T, LANES), F32), jax.ShapeDtypeStruct((ROUTE_ROWS, T), F32),
                   jax.ShapeDtypeStruct((ROUTE_ROWS, n_win), F32)],
        compiler_params=_params("arbitrary"),
        name="assign_slots",
    )(group_offset, meta)


def _flatten_items(n_items, n_slots):
    i32 = jnp.int32
    end = jnp.cumsum(n_items).astype(i32)
    start = end - n_items
    total = end[-1]
    w = jnp.arange(n_slots, dtype=i32)
    wc = jnp.minimum(w, total - 1)
    owner = jnp.minimum(jnp.sum(wc[:, None] >= end[None, :], axis=1), n_items.shape[0] - 1).astype(i32)
    return owner, wc - start[owner], w < total, wc, start


def _moe_plan(cum, counts, *, n_experts, expert_base):
    i32 = jnp.int32
    T, E = cum.shape[0], n_experts
    cnt = counts[0, :E].astype(i32)
    tiles_e = (cnt + SLOT_TILE - 1) // SLOT_TILE
    tile_end_e = jnp.cumsum(tiles_e).astype(i32)
    tile_start_e = tile_end_e - tiles_e
    n_used = tile_end_e[-1]
    off_e = tile_start_e * SLOT_TILE

    n_tiles = 2 * T // SLOT_TILE + E
    r = jnp.arange(n_tiles, dtype=i32)
    tile_expert = jnp.minimum(jnp.sum(r[:, None] >= tile_end_e[None, :], axis=1), E - 1).astype(i32)

    n_tok_tiles = T // COMBINE_TILE
    cum_b = cum[::COMBINE_TILE, :E].astype(i32)
    cum_n = jnp.concatenate([cum_b[1:], cnt[None, :]], axis=0)
    s_a, s_b = off_e[None, :] + cum_b, off_e[None, :] + cum_n
    p_lo, p_hi = s_a // COMBINE_PIECE, (s_b - 1) // COMBINE_PIECE
    n_piece = jnp.where(s_b > s_a, p_hi - p_lo + 1, 0).astype(i32)
    piece_end = jnp.cumsum(n_piece, axis=1).astype(i32)
    piece_start = piece_end - n_piece
    n_valid = piece_end[:, -1]
    kk = jnp.minimum(jnp.arange(_combine_pieces(E), dtype=i32)[None, :], n_valid[:, None] - 1)
    owner = jnp.sum(kk[:, :, None] >= piece_end[:, None, :], axis=2).astype(i32)
    is_owner = owner[:, :, None] == jnp.arange(E, dtype=i32)[None, None, :]
    take = lambda a: jnp.sum(jnp.where(is_owner, a[:, None, :], 0), axis=2)
    piece = take(p_lo) + kk - take(piece_start)
    comb = (piece.reshape(-1).astype(i32), n_valid)

    return dict(group_offset=off_e, comb=comb, tile_expert=tile_expert + expert_base,
                n_used=n_used, n_tiles=n_tiles)


def _dispatch_plan(win, n_used, n_tiles, n_tokens, n_experts):
    i32 = jnp.int32
    P = DISPATCH_PIECES
    per = SLOT_TILE // DISPATCH_TILE
    n_disp_tiles = n_tiles * per
    n_blocks = n_tokens // DISPATCH_PIECE
    hi = win[1, :n_disp_tiles].astype(i32)
    lo = jnp.where(hi >= 0, win[0, :n_disp_tiles], 0.0).astype(i32)
    n_p = jnp.where(hi >= 0, hi - lo + 1, 0)
    used = (jnp.arange(n_disp_tiles, dtype=i32) // per) < n_used
    n_it = jnp.where(used, jnp.maximum(1, (n_p + P - 1) // P), 0).astype(i32)
    n_items = (n_experts * n_blocks + n_disp_tiles) // P + 1 + n_disp_tiles
    owner, k, real, wc, start = _flatten_items(n_it, n_items)
    first = lo[owner] + k * P
    n_valid = jnp.where(real, jnp.clip(n_p[owner] - k * P, 0, P), 0).astype(i32)
    flag = jnp.where(real, jnp.where(k == 0, 1, 2), 0).astype(i32)
    blocks = jnp.clip(first[:, None] + jnp.arange(P, dtype=i32)[None, :], 0, n_blocks - 1)
    return owner, flag, n_valid, blocks.reshape(-1).astype(i32)


def _dispatch_kernel(tile_ref, flag_ref, nv_ref, block_ref, *refs, n_pieces):
    rows_refs, hn_refs, o_ref = refs[:n_pieces], refs[n_pieces:2 * n_pieces], refs[2 * n_pieces]
    w = pl.program_id(0)
    flag = flag_ref[w]
    rows, cols = o_ref.shape[0], hn_refs[0].shape[0]
    row = lax.broadcasted_iota(jnp.int32, (rows, cols), 0)

    def one_hot(j):
        base = jnp.where(j < nv_ref[w], tile_ref[w] * rows, -2 * rows)
        slot = (base + row).astype(F32)
        hit = (rows_refs[j][_POS1:_POS1 + 1, :] == slot) | (rows_refs[j][_POS2:_POS2 + 1, :] == slot)
        return jnp.where(hit, 1.0, 0.0).astype(BF16)

    def gathered(j):
        sel = jnp.concatenate([one_hot(j), one_hot(j + 1)], axis=1)
        hn = jnp.concatenate([hn_refs[j][...], hn_refs[j + 1][...]], axis=0)
        return _dot(sel, hn).astype(BF16)

    @pl.when(flag == 1)
    def _():
        o_ref[...] = gathered(0)

    @pl.when(flag == 2)
    def _():
        o_ref[...] += gathered(0)

    for j in range(2, n_pieces, 2):
        @pl.when((flag > 0) & (j < nv_ref[w]))
        def _():
            o_ref[...] += gathered(j)


def dispatch_tokens(hn, rows, items, n_slots):
    T, D = hn.shape
    tile, flag, n_valid, blocks = items
    P = DISPATCH_PIECES
    rows_spec = lambda j: pl.BlockSpec((ROUTE_ROWS, DISPATCH_PIECE), lambda w, t, f, nv, b: (0, b[w * P + j]))
    hn_spec = lambda j: pl.BlockSpec((DISPATCH_PIECE, D), lambda w, t, f, nv, b: (b[w * P + j], 0))
    return pl.pallas_call(
        functools.partial(_dispatch_kernel, n_pieces=P),
        grid_spec=pltpu.PrefetchScalarGridSpec(
            num_scalar_prefetch=4,
            grid=(tile.shape[0],),
            in_specs=[rows_spec(j) for j in range(P)] + [hn_spec(j) for j in range(P)],
            out_specs=pl.BlockSpec((DISPATCH_TILE, D), lambda w, t, f, nv, b: (t[w], 0)),
        ),
        out_shape=jax.ShapeDtypeStruct((n_slots, D), BF16),
        compiler_params=_params("arbitrary"),
        name="moe_dispatch",
    )(tile, flag, n_valid, blocks, *([rows] * P), *([hn] * P))


def _expert_kernel(te_ref, nu_ref, x_ref, wg_ref, wu_ref, wd_ref, o_ref, acc_ref, *, sub):
    r, f = pl.program_id(0), pl.program_id(1)

    @pl.when(f == 0)
    def _():
        acc_ref[...] = jnp.zeros_like(acc_ref)

    @pl.when(r < nu_ref[0])
    def _():
        x = x_ref[...]
        hidden = []
        for c in range(wg_ref.shape[2] // sub):
            cols = slice(c * sub, (c + 1) * sub)
            a = _dot(x, wg_ref[0, :, cols])
            b = _dot(x, wu_ref[0, :, cols])
            hidden.append((a * jax.nn.sigmoid(a) * b).astype(BF16))
        acc_ref[...] += _dot(jnp.concatenate(hidden, axis=1), wd_ref[0])

    @pl.when(f == pl.num_programs(1) - 1)
    def _():
        o_ref[...] = acc_ref[...].astype(BF16)


def expert_swiglu(xs, plan, w_gate, w_up, w_down, *, tf=1792, sub=256):
    S, D = xs.shape
    FF = w_gate.shape[2]
    tf = tf if FF % tf == 0 else FF
    sub = min(sub, tf)
    assert FF % tf == 0 and tf % sub == 0
    nf = FF // tf
    last = lambda r, nu: jnp.minimum(r, nu[0] - 1)
    col = lambda r, f, nu: jnp.where(r < nu[0], f, nf - 1)
    return pl.pallas_call(
        functools.partial(_expert_kernel, sub=sub),
        grid_spec=pltpu.PrefetchScalarGridSpec(
            num_scalar_prefetch=2,
            grid=(S // SLOT_TILE, nf),
            in_specs=[pl.BlockSpec((SLOT_TILE, D), lambda r, f, te, nu: (last(r, nu), 0)),
                      pl.BlockSpec((1, D, tf), lambda r, f, te, nu: (te[last(r, nu)], 0, col(r, f, nu))),
                      pl.BlockSpec((1, D, tf), lambda r, f, te, nu: (te[last(r, nu)], 0, col(r, f, nu))),
                      pl.BlockSpec((1, tf, D), lambda r, f, te, nu: (te[last(r, nu)], col(r, f, nu), 0))],
            out_specs=pl.BlockSpec((SLOT_TILE, D), lambda r, f, te, nu: (r, 0)),
            scratch_shapes=[pltpu.VMEM((SLOT_TILE, D), F32)],
        ),
        out_shape=jax.ShapeDtypeStruct((S, D), BF16),
        compiler_params=_params("arbitrary", "arbitrary"),
        name="moe_experts",
    )(plan["tile_expert"], plan["n_used"].reshape(1), xs, w_gate, w_up, w_down)


def _combine_pieces(n_experts):
    return 2 * COMBINE_TILE // COMBINE_PIECE + 2 * n_experts


def _combine_kernel(piece_ref, nv_ref, rec_ref, x_ref, g_ref, *rest, n_pieces, normalize):
    y_refs, o_ref = rest[:n_pieces], rest[n_pieces]
    i = pl.program_id(0)
    rows = o_ref.shape[0]
    rec = rec_ref[...]
    pos1, pos2 = rec[:, _POS1:_POS1 + 1], rec[:, _POS2:_POS2 + 1]
    gate1, gate2 = rec[:, _GATE1:_GATE1 + 1], rec[:, _GATE2:_GATE2 + 1]
    lane = lax.broadcasted_iota(jnp.int32, (rows, COMBINE_PIECE), 1)

    def weights(k):
        base = jnp.where(k < nv_ref[i], piece_ref[i * n_pieces + k] * COMBINE_PIECE, -COMBINE_PIECE)
        slot = (base + lane).astype(F32)
        return (jnp.where(pos1 == slot, gate1, 0.0) + jnp.where(pos2 == slot, gate2, 0.0)).astype(BF16)

    o_ref[...] = x_ref[...]
    for k in range(0, n_pieces, 2):
        @pl.when(k < nv_ref[i])
        def _():
            sel = jnp.concatenate([weights(k), weights(k + 1)], axis=1)
            y = jnp.concatenate([y_refs[k][...], y_refs[k + 1][...]], axis=0)
            o_ref[...] += _dot(sel, y)

    if normalize:
        o_ref[...] = _rms(o_ref[...], g_ref[...])


def combine_tokens(x, y, rec, plan, *, n_experts, final_g=None):
    T, D = x.shape
    piece, n_valid = plan["comb"]
    n_pieces = _combine_pieces(n_experts)
    g = jnp.ones((1, D), F32) if final_g is None else final_g.reshape(1, D).astype(F32)
    y_spec = lambda k: pl.BlockSpec((COMBINE_PIECE, D), lambda i, p, nv: (p[i * n_pieces + k], 0))
    return pl.pallas_call(
        functools.partial(_combine_kernel, n_pieces=n_pieces, normalize=final_g is not None),
        grid_spec=pltpu.PrefetchScalarGridSpec(
            num_scalar_prefetch=2,
            grid=(T // COMBINE_TILE,),
            in_specs=[pl.BlockSpec((COMBINE_TILE, LANES), lambda i, p, nv: (i, 0)),
                      pl.BlockSpec((COMBINE_TILE, D), lambda i, p, nv: (i, 0)),
                      pl.BlockSpec((1, D), lambda i, p, nv: (0, 0))] + [y_spec(k) for k in range(n_pieces)],
            out_specs=pl.BlockSpec((COMBINE_TILE, D), lambda i, p, nv: (i, 0)),
        ),
        out_shape=jax.ShapeDtypeStruct((T, D), F32),
        compiler_params=_params("parallel"),
        name="moe_combine",
    )(piece, n_valid, rec, x, g, *([y] * n_pieces))


def moe_residual(x, g, router_w, w_gate, w_up, w_down, layer, *, final_g=None):
    E = router_w.shape[1]
    hn, meta, cum, counts = route_tokens(x, g, router_w)
    plan = _moe_plan(cum, counts, n_experts=E, expert_base=layer * E)
    n_tiles = plan["n_tiles"]
    rec, rows, win = assign_slots(meta, plan["group_offset"], n_experts=E,
                                  n_dispatch_tiles=n_tiles * (SLOT_TILE // DISPATCH_TILE))
    items = _dispatch_plan(win, plan["n_used"], n_tiles, x.shape[0], E)
    xs = dispatch_tokens(hn, rows, items, n_tiles * SLOT_TILE)
    y = expert_swiglu(xs, plan, w_gate, w_up, w_down)
    return combine_tokens(x, y, rec, plan, n_experts=E, final_g=final_g)


def _final_norm_kernel(x_ref, g_ref, o_ref):
    o_ref[...] = _rms(x_ref[...], g_ref[...])


def final_norm(x, g, *, tm=1024):
    T, D = x.shape
    tm = min(tm, T)
    return pl.pallas_call(
        _final_norm_kernel,
        grid=(T // tm,),
        in_specs=[pl.BlockSpec((tm, D), lambda i: (i, 0)), pl.BlockSpec((1, D), lambda i: (0, 0))],
        out_specs=pl.BlockSpec((tm, D), lambda i: (i, 0)),
        out_shape=jax.ShapeDtypeStruct((T, D), F32),
        compiler_params=_params("parallel"),
        name="final_norm",
    )(x, g.reshape(1, D))


def kernel(x, mem, positions, mem_norm_g, final_norm_g, mix_norm_g, xa_norm_g, ffn_norm_g, xa_wq, xa_wkv, xa_wo, ev_w_in, ev_conv_w, ev_ret_decay_f, ev_ret_decay_b, ev_ret_gn_g, ev_w_out, ffd_w_gate, ffd_w_up, ffd_w_down, od_w_in, od_lam_q1, od_lam_k1, od_lam_q2, od_lam_k2, od_gn_g, od_w_out, moe_router, moe_w_gate, moe_w_up, moe_w_down):
    B, L, D = x.shape
    M = mem.shape[1]
    T = B * L
    depth = mix_norm_g.shape[0]
    conv_ch = ev_conv_w.shape[2]
    ret_w = ev_ret_gn_g.shape[1]
    ret_dk = ret_w // RET_HEADS
    diff_d = od_lam_q1.shape[1]
    bf = lambda w: w.astype(BF16)
    experts = lambda w: bf(w).reshape((-1,) + w.shape[2:])

    xa_wq, xa_wkv, xa_wo = bf(xa_wq), bf(xa_wkv), bf(xa_wo)
    ev_w_in, ev_w_out, od_w_in, od_w_out = bf(ev_w_in), bf(ev_w_out), bf(od_w_in), bf(od_w_out)
    ffd_w_gate, ffd_w_up, ffd_w_down = bf(ffd_w_gate), bf(ffd_w_up), bf(ffd_w_down)
    moe_w_gate, moe_w_up, moe_w_down = experts(moe_w_gate), experts(moe_w_up), experts(moe_w_down)
    ev_conv_w = ev_conv_w.astype(F32)

    ret_cos, ret_sin = _rope_tables(positions, ret_dk, RET_THETA)
    diff_cos, diff_sin = _rope_tables(positions, diff_d // 4, ROPE_THETA)
    mem2 = mem.reshape(B * M, D)
    xt = x.reshape(T, D)

    for i in range(depth):
        j = i // 2
        if i % 2 == 0:
            proj = norm_proj(xt, mix_norm_g[i], ev_w_in, j).reshape(B, L, -1)
            a = short_conv(proj, ev_conv_w, j, ch=conv_ch)
            r = retention(proj, ret_cos, ret_sin, ev_ret_decay_f[j], ev_ret_decay_b[j], ev_ret_gn_g[j],
                          col0=3 * conv_ch // ret_w, width=ret_w, dk=ret_dk)
            parts, w_out = [a, r], ev_w_out
        else:
            lambda_init = 0.8 - 0.6 * math.exp(-0.3 * i)
            f32 = lambda v: v.astype(F32)
            lam = (jnp.exp(jnp.sum(f32(od_lam_q1[j]) * f32(od_lam_k1[j])))
                   - jnp.exp(jnp.sum(f32(od_lam_q2[j]) * f32(od_lam_k2[j]))) + lambda_init)
            proj = norm_proj(xt, mix_norm_g[i], od_w_in, j).reshape(B, L, -1)
            o = diff_attention(proj, diff_cos, diff_sin, lam, od_gn_g[j], rot_dim=diff_d // 4,
                               lambda_init=lambda_init, d=diff_d)
            parts, w_out = [o], od_w_out

        kv = norm_proj(mem2, mem_norm_g, xa_wkv, i).reshape(B, M, 2 * D)
        xt = mixer_out_cross_attention(xt.reshape(B, L, D), parts, w_out, j, xa_norm_g[i], xa_wq, kv, xa_wo,
                                       i).reshape(T, D)

        if i % 2 == 0:
            xt = swiglu_residual(xt, ffn_norm_g[i], ffd_w_gate, ffd_w_up, ffd_w_down, j)
        else:
            xt = moe_residual(xt, ffn_norm_g[i], moe_router[j], moe_w_gate, moe_w_up, moe_w_down, j,
                              final_g=final_norm_g if i == depth - 1 else None)

    if depth % 2 == 1:
        xt = final_norm(xt, final_norm_g)
    return xt.reshape(B, L, D)
```
